```python
import jax, jax.numpy as jnp
from jax import lax
import numpy as np

D_MODEL = 2048
BATCH = 1
SEQ = 16384
DEPTH = 1
DEC_BATCH = 32
DEC_SEQ = 1
PAST_LEN = 16384
PAGE_SIZE = 128

HEAD_DIM = 128
N_HEADS = D_MODEL // HEAD_DIM
N_FOX = N_HEADS // 2
N_SB = N_HEADS - N_FOX
D_FOX = N_FOX * HEAD_DIM
D_SB = N_SB * HEAD_DIM
D_MIX = D_FOX + D_SB
OFF_FQ = 0
OFF_FK = OFF_FQ + D_FOX
OFF_FV = OFF_FK + D_FOX
OFF_FF = OFF_FV + D_FOX
OFF_SQ = OFF_FF + N_FOX
OFF_SK = OFF_SQ + D_SB
OFF_SV = OFF_SK + D_SB
D_IN = OFF_SV + D_SB
ATTN_SCALE = HEAD_DIM ** -0.5
Q_BLOCK = 128
N_EXPERTS = 32
TOP_K = 4
D_EXPERT = D_MODEL
SWIGLU_ALPHA = 1.702
SWIGLU_LIMIT = 7.0
MOE_BLOCK = 512
MOE_MIN_BLOCK = 8
NORM_EPS = 1e-6
FORGET_BIAS_LO = 4.0
FORGET_BIAS_HI = 8.0
SB_BIAS_LO = -8.0
SB_BIAS_HI = -5.0

kernel_name = 'hybrid_fox_stickbreak_moe_step'


def rms_norm(x, g):
    xf = x.astype(jnp.float32)
    y = xf * lax.rsqrt(jnp.mean(xf * xf, axis=-1, keepdims=True) + NORM_EPS)
    return (y * g.astype(jnp.float32)).astype(x.dtype)


def rev_cumsum_excl(a, axis):
    return lax.cumsum(a, axis=axis, reverse=True) - a


def split_projection(z, b_forget_l):
    lead = z.shape[:-1]
    fq = z[..., OFF_FQ:OFF_FK].reshape(lead + (N_FOX, HEAD_DIM))
    fk = z[..., OFF_FK:OFF_FV].reshape(lead + (N_FOX, HEAD_DIM))
    fv = z[..., OFF_FV:OFF_FF].reshape(lead + (N_FOX, HEAD_DIM))
    logf = jax.nn.log_sigmoid(z[..., OFF_FF:OFF_SQ].astype(jnp.float32) + b_forget_l.astype(jnp.float32))
    sq = z[..., OFF_SQ:OFF_SK].reshape(lead + (N_SB, HEAD_DIM))
    sk = z[..., OFF_SK:OFF_SV].reshape(lead + (N_SB, HEAD_DIM))
    sv = z[..., OFF_SV:D_IN].reshape(lead + (N_SB, HEAD_DIM))
    return fq, fk, fv, logf, sq, sk, sv


def fox_prompt(q, k, v, logf):
    b, s, h, dh = q.shape
    qf = q.astype(jnp.float32)
    kf = k.astype(jnp.float32)
    vf = v.astype(jnp.float32)
    big_f = jnp.cumsum(logf.astype(jnp.float32), axis=1).transpose(0, 2, 1)
    kpos = jnp.arange(s)

    def block(i):
        s0 = i * Q_BLOCK
        qb = lax.dynamic_slice_in_dim(qf, s0, Q_BLOCK, axis=1)
        fq = lax.dynamic_slice_in_dim(big_f, s0, Q_BLOCK, axis=2)
        qpos = s0 + jnp.arange(Q_BLOCK)
        logits = (jnp.einsum('bqhd,bkhd->bhqk', qb, kf) * ATTN_SCALE
                  + fq[:, :, :, None] - big_f[:, :, None, :])
        logits = jnp.where(kpos[None, :] <= qpos[:, None], logits, -jnp.inf)
        p = jax.nn.softmax(logits, axis=-1)
        return jnp.einsum('bhqk,bkhd->bqhd', p, vf)

    out = lax.map(block, jnp.arange(s // Q_BLOCK))
    return out.transpose(1, 0, 2, 3, 4).reshape(b, s, h, dh).astype(q.dtype)


def sb_prompt(q, k, v, sb_bias):
    b, s, h, dh = q.shape
    qf = q.astype(jnp.float32)
    kf = k.astype(jnp.float32)
    vf = v.astype(jnp.float32)
    bias = sb_bias.astype(jnp.float32)[None, :, None, None]
    kpos = jnp.arange(s)

    def block(i):
        s0 = i * Q_BLOCK
        qb = lax.dynamic_slice_in_dim(qf, s0, Q_BLOCK, axis=1)
        qpos = s0 + jnp.arange(Q_BLOCK)
        z = jnp.einsum('bqhd,bkhd->bhqk', qb, kf) * ATTN_SCALE + bias
        valid = kpos[None, :] < qpos[:, None]
        log_keep = jnp.where(valid, jax.nn.log_sigmoid(-z), 0.0)
        a = jnp.where(valid, jnp.exp(jax.nn.log_sigmoid(z) + rev_cumsum_excl(log_keep, 3)), 0.0)
        return jnp.einsum('bhqk,bkhd->bqhd', a, vf)

    out = lax.map(block, jnp.arange(s // Q_BLOCK))
    return out.transpose(1, 0, 2, 3, 4).reshape(b, s, h, dh).astype(q.dtype)


def prompt_attend(fq, fk, fv, logf, sq, sk, sv, sb_bias):
    return fox_prompt(fq, fk, fv, logf), sb_prompt(sq, sk, sv, sb_bias)


def sample_attend(l, fq, fk, fv, logf, sq, sk, sv, sb_bias,
                  cache_fox_k, cache_fox_v, cache_fox_logf, cache_sb_k, cache_sb_v, page_table):
    n, t, h, dh = fq.shape
    f32 = jnp.float32
    qf = fq.astype(f32).transpose(0, 2, 1, 3)
    kf = fk.astype(f32).transpose(0, 2, 1, 3)
    vf = fv.astype(f32).transpose(0, 2, 1, 3)
    qs = sq.astype(f32).transpose(0, 2, 1, 3)
    ks = sk.astype(f32).transpose(0, 2, 1, 3)
    vs = sv.astype(f32).transpose(0, 2, 1, 3)
    bias = sb_bias.astype(f32)[None, :, None, None]
    fn = jnp.cumsum(logf.astype(f32), axis=1).transpose(0, 2, 1)
    ii = jnp.arange(t)
    causal = ii[None, :] <= ii[:, None]
    strict = ii[None, :] < ii[:, None]
    s_loc = jnp.einsum('bhqd,bhkd->bhqk', qf, kf) * ATTN_SCALE + fn[..., :, None] - fn[..., None, :]
    s_loc = jnp.where(causal, s_loc, -jnp.inf)
    m0 = jnp.max(s_loc, axis=-1)
    p_loc = jnp.exp(s_loc - m0[..., None])
    l0 = jnp.sum(p_loc, axis=-1)
    acc0 = jnp.einsum('bhqk,bhkd->bhqd', p_loc, vf)
    d0 = jnp.zeros((n, h), f32)
    z_loc = jnp.einsum('bhqd,bhkd->bhqk', qs, ks) * ATTN_SCALE + bias
    keep_loc = jnp.where(strict, jax.nn.log_sigmoid(-z_loc), 0.0)
    a_loc = jnp.where(strict, jnp.exp(jax.nn.log_sigmoid(z_loc) + rev_cumsum_excl(keep_loc, 3)), 0.0)
    sb_acc0 = jnp.einsum('bhqk,bhkd->bhqd', a_loc, vs)
    sb_keep0 = jnp.sum(keep_loc, axis=-1)

    def page_step(carry, pt):
        m, lsum, acc, dsuf, sb_acc, sb_keep = carry
        kp = cache_fox_k[l, pt].astype(f32)
        vp = cache_fox_v[l, pt].astype(f32)
        lf = cache_fox_logf[l, pt].astype(f32).transpose(0, 2, 1)
        suf = rev_cumsum_excl(lf, 2) + dsuf[..., None]
        sc = (jnp.einsum('bhqd,bkhd->bhqk', qf, kp) * ATTN_SCALE
              + fn[..., :, None] + suf[:, :, None, :])
        m_new = jnp.maximum(m, jnp.max(sc, axis=-1))
        corr = jnp.exp(m - m_new)
        p = jnp.exp(sc - m_new[..., None])
        lsum = lsum * corr + jnp.sum(p, axis=-1)
        acc = acc * corr[..., None] + jnp.einsum('bhqk,bkhd->bhqd', p, vp)
        dsuf = dsuf + jnp.sum(lf, axis=-1)
        kps = cache_sb_k[l, pt].astype(f32)
        vps = cache_sb_v[l, pt].astype(f32)
        zp = jnp.einsum('bhqd,bkhd->bhqk', qs, kps) * ATTN_SCALE + bias
        keep = jax.nn.log_sigmoid(-zp)
        a = jnp.exp(jax.nn.log_sigmoid(zp) + rev_cumsum_excl(keep, 3) + sb_keep[..., None])
        sb_acc = sb_acc + jnp.einsum('bhqk,bkhd->bhqd', a, vps)
        sb_keep = sb_keep + jnp.sum(keep, axis=-1)
        return (m_new, lsum, acc, dsuf, sb_acc, sb_keep), None

    pages_rev = page_table.T[::-1]
    (m, lsum, acc, dsuf, sb_acc, sb_keep), _ = lax.scan(
        page_step, (m0, l0, acc0, d0, sb_acc0, sb_keep0), pages_rev)
    o_fox = (acc / lsum[..., None]).transpose(0, 2, 1, 3).astype(fq.dtype)
    o_sb = sb_acc.transpose(0, 2, 1, 3).astype(sq.dtype)
    return o_fox, o_sb


def moe_ffn(h, l, w_router, b_router, w_gate_up, b_gate_up, w_down, b_down):
    lead = h.shape[:-1]
    x = h.reshape(-1, D_MODEL)
    n_tok = x.shape[0]
    f32 = jnp.float32
    logits = (x @ w_router[l]).astype(f32) + b_router[l].astype(f32)
    top_val, top_idx = lax.top_k(logits, TOP_K)
    gates = jax.nn.softmax(top_val, axis=-1)
    m = n_tok * TOP_K
    blk = MOE_BLOCK
    while blk > MOE_MIN_BLOCK and blk * N_EXPERTS > m:
        blk //= 2
    n_blk = -(-m // blk) + min(N_EXPERTS, m)
    flat_e = top_idx.reshape(-1).astype(jnp.int32)
    flat_tok = jnp.repeat(jnp.arange(n_tok, dtype=jnp.int32), TOP_K)
    flat_g = gates.reshape(-1)
    order = jnp.argsort(flat_e)
    se = flat_e[order]
    counts = jnp.bincount(flat_e, length=N_EXPERTS)
    padded = (counts + blk - 1) // blk * blk
    starts = jnp.cumsum(counts) - counts
    pstarts = jnp.cumsum(padded) - padded
    dest = pstarts[se] + jnp.arange(m, dtype=jnp.int32) - starts[se]
    slot_tok = jnp.zeros((n_blk * blk,), jnp.int32).at[dest].set(flat_tok[order])
    slot_gate = jnp.zeros((n_blk * blk,), f32).at[dest].set(flat_g[order])
    pends = jnp.cumsum(padded)
    bstart = jnp.arange(n_blk) * blk
    blk_e = jnp.minimum(jnp.sum(pends[None, :] <= bstart[:, None], axis=1), N_EXPERTS - 1).astype(jnp.int32)

    def expert_block(acc, inp):
        tok, gate, e = inp
        xb = x[tok]
        hu = (xb @ w_gate_up[l, e]).astype(f32) + b_gate_up[l, e].astype(f32)
        g = jnp.minimum(hu[:, :D_EXPERT], SWIGLU_LIMIT)
        u = jnp.clip(hu[:, D_EXPERT:], -SWIGLU_LIMIT, SWIGLU_LIMIT)
        act = g * jax.nn.sigmoid(SWIGLU_ALPHA * g) * (u + 1.0)
        y = (act.astype(x.dtype) @ w_down[l, e]).astype(f32) + b_down[l, e].astype(f32)
        return acc.at[tok].add(y * gate[:, None]), None

    out, _ = lax.scan(expert_block, jnp.zeros((n_tok, D_MODEL), f32),
                      (slot_tok.reshape(n_blk, blk), slot_gate.reshape(n_blk, blk), blk_e))
    return out.reshape(lead + (D_MODEL,)).astype(h.dtype)


def trunk_layer(x, c, l, attend, w_ada, b_ada, g_pre_mix, w_in, b_forget, b_sb, g_fox_out, g_sb_out, w_out,
                g_post_mix, g_pre_ffn, w_router, b_router, w_gate_up, b_gate_up, w_down, b_down, g_post_ffn):
    f32 = jnp.float32
    lead = x.shape[:-1]
    mod = jax.nn.silu(c.astype(f32)) @ w_ada[l].astype(f32) + b_ada[l].astype(f32)
    sh_a, sc_a, gt_a, sh_m, sc_m, gt_m = jnp.split(mod[:, None, :], 6, axis=-1)
    h = (rms_norm(x, g_pre_mix[l]).astype(f32) * (1.0 + sc_a) + sh_a).astype(x.dtype)
    z = h @ w_in[l]
    fq, fk, fv, logf, sq, sk, sv = split_projection(z, b_forget[l])
    o_fox, o_sb = attend(fq, fk, fv, logf, sq, sk, sv, b_sb[l])
    o = jnp.concatenate([rms_norm(o_fox, g_fox_out[l]).reshape(lead + (D_FOX,)),
                         rms_norm(o_sb, g_sb_out[l]).reshape(lead + (D_SB,))], axis=-1)
    mix = o @ w_out[l]
    x = (x.astype(f32) + gt_a * rms_norm(mix, g_post_mix[l]).astype(f32)).astype(x.dtype)
    h = (rms_norm(x, g_pre_ffn[l]).astype(f32) * (1.0 + sc_m) + sh_m).astype(x.dtype)
    ffn = moe_ffn(h, l, w_router, b_router, w_gate_up, b_gate_up, w_down, b_down)
    x = (x.astype(f32) + gt_m * rms_norm(ffn, g_post_ffn[l]).astype(f32)).astype(x.dtype)
    return x, (fk, fv, logf, sk, sv)


def setup_inputs(seed: int = 0) -> dict:
    key = jax.random.key(seed)
    ks = jax.random.split(key, 32)
    f32 = jnp.float32
    n_pages = PAST_LEN // PAGE_SIZE
    n_used = DEC_BATCH * n_pages
    n_pool = n_used + (n_used + 3) // 4

    def nrm(k, shape, scale):
        return jax.random.normal(k, shape, f32) * scale

    def gain(k, shape):
        return 1.0 + 0.05 * jax.random.normal(k, shape, f32)

    x_prompt = nrm(ks[0], (BATCH, SEQ, D_MODEL), 1.0)
    x_sample = nrm(ks[1], (DEC_BATCH, DEC_SEQ, D_MODEL), 1.0)
    cache_fox_k = nrm(ks[2], (DEPTH, n_pool, PAGE_SIZE, N_FOX, HEAD_DIM), 1.0)
    cache_fox_v = nrm(ks[3], (DEPTH, n_pool, PAGE_SIZE, N_FOX, HEAD_DIM), 1.0)
    cache_fox_logf = jax.nn.log_sigmoid(
        jax.random.uniform(ks[4], (DEPTH, n_pool, PAGE_SIZE, N_FOX), f32, FORGET_BIAS_LO, FORGET_BIAS_HI)
        + nrm(ks[5], (DEPTH, n_pool, PAGE_SIZE, N_FOX), 0.5))
    cache_sb_k = nrm(ks[6], (DEPTH, n_pool, PAGE_SIZE, N_SB, HEAD_DIM), 1.0)
    cache_sb_v = nrm(ks[7], (DEPTH, n_pool, PAGE_SIZE, N_SB, HEAD_DIM), 1.0)
    page_table = jax.random.permutation(ks[8], n_pool)[:n_used].astype(jnp.int32).reshape(DEC_BATCH, n_pages)
    c_prompt = nrm(ks[9], (BATCH, D_MODEL), 1.0)
    c_sample = nrm(ks[10], (DEC_BATCH, D_MODEL), 1.0)
    w_ada = nrm(ks[11], (DEPTH, D_MODEL, 6 * D_MODEL), 0.3 * D_MODEL ** -0.5)
    b_ada = nrm(ks[12], (DEPTH, 6 * D_MODEL), 0.02)
    g_pre_mix = gain(ks[13], (DEPTH, D_MODEL))
    w_in = nrm(ks[14], (DEPTH, D_MODEL, D_IN), D_MODEL ** -0.5)
    b_forget = jax.random.uniform(ks[15], (DEPTH, N_FOX), f32, FORGET_BIAS_LO, FORGET_BIAS_HI)
    b_sb = jax.random.uniform(ks[28], (DEPTH, N_SB), f32, SB_BIAS_LO, SB_BIAS_HI)
    g_fox_out = gain(ks[16], (DEPTH, N_FOX, HEAD_DIM))
    g_sb_out = gain(ks[17], (DEPTH, N_SB, HEAD_DIM))
    w_out = nrm(ks[18], (DEPTH, D_MIX, D_MODEL), D_MIX ** -0.5)
    g_post_mix = gain(ks[19], (DEPTH, D_MODEL))
    g_pre_ffn = gain(ks[20], (DEPTH, D_MODEL))
    w_router = nrm(ks[21], (DEPTH, D_MODEL, N_EXPERTS), D_MODEL ** -0.5)
    b_router = nrm(ks[22], (DEPTH, N_EXPERTS), 0.01)
    w_gate_up = nrm(ks[23], (DEPTH, N_EXPERTS, D_MODEL, 2 * D_EXPERT), D_MODEL ** -0.5)
    b_gate_up = nrm(ks[24], (DEPTH, N_EXPERTS, 2 * D_EXPERT), 0.02)
    w_down = nrm(ks[25], (DEPTH, N_EXPERTS, D_EXPERT, D_MODEL), D_EXPERT ** -0.5)
    b_down = nrm(ks[26], (DEPTH, N_EXPERTS, D_MODEL), 0.02)
    g_post_ffn = gain(ks[27], (DEPTH, D_MODEL))
    return {'x_prompt': x_prompt, 'x_sample': x_sample,
            'cache_fox_k': cache_fox_k, 'cache_fox_v': cache_fox_v, 'cache_fox_logf': cache_fox_logf,
            'cache_sb_k': cache_sb_k, 'cache_sb_v': cache_sb_v, 'page_table': page_table,
            'c_prompt': c_prompt, 'c_sample': c_sample,
            'w_ada': w_ada, 'b_ada': b_ada, 'g_pre_mix': g_pre_mix, 'w_in': w_in, 'b_forget': b_forget,
            'b_sb': b_sb, 'g_fox_out': g_fox_out, 'g_sb_out': g_sb_out, 'w_out': w_out,
            'g_post_mix': g_post_mix, 'g_pre_ffn': g_pre_ffn, 'w_router': w_router, 'b_router': b_router,
            'w_gate_up': w_gate_up, 'b_gate_up': b_gate_up, 'w_down': w_down, 'b_down': b_down,
            'g_post_ffn': g_post_ffn}


def reference(x_prompt, x_sample, cache_fox_k, cache_fox_v, cache_fox_logf, cache_sb_k, cache_sb_v, page_table,
              c_prompt, c_sample, w_ada, b_ada, g_pre_mix, w_in, b_forget, b_sb, g_fox_out, g_sb_out, w_out,
              g_post_mix, g_pre_ffn, w_router, b_router, w_gate_up, b_gate_up, w_down, b_down, g_post_ffn):
    xp = x_prompt
    xs = x_sample
    st_p = []
    st_s = []
    for l in range(DEPTH):
        xp, new_p = trunk_layer(xp, c_prompt, l, prompt_attend, w_ada, b_ada, g_pre_mix, w_in, b_forget, b_sb,
                                g_fox_out, g_sb_out, w_out, g_post_mix, g_pre_ffn, w_router, b_router,
                                w_gate_up, b_gate_up, w_down, b_down, g_post_ffn)

        def sample_fn(fq, fk, fv, logf, sq, sk, sv, sbb, l=l):
            return sample_attend(l, fq, fk, fv, logf, sq, sk, sv, sbb, cache_fox_k, cache_fox_v,
                                 cache_fox_logf, cache_sb_k, cache_sb_v, page_table)

        xs, new_s = trunk_layer(xs, c_sample, l, sample_fn, w_ada, b_ada, g_pre_mix, w_in, b_forget, b_sb,
                                g_fox_out, g_sb_out, w_out, g_post_mix, g_pre_ffn, w_router, b_router,
                                w_gate_up, b_gate_up, w_down, b_down, g_post_ffn)
        st_p.append(new_p)
        st_s.append(new_s)
    fox_k_prompt = jnp.stack([s[0] for s in st_p])
    fox_v_prompt = jnp.stack([s[1] for s in st_p])
    fox_logf_prompt = jnp.stack([s[2] for s in st_p])
    sb_k_prompt = jnp.stack([s[3] for s in st_p])
    sb_v_prompt = jnp.stack([s[4] for s in st_p])
    fox_k_sample = jnp.stack([s[0] for s in st_s])
    fox_v_sample = jnp.stack([s[1] for s in st_s])
    fox_logf_sample = jnp.stack([s[2] for s in st_s])
    sb_k_sample = jnp.stack([s[3] for s in st_s])
    sb_v_sample = jnp.stack([s[4] for s in st_s])
    return (xp, xs, fox_k_prompt, fox_v_prompt, fox_logf_prompt, sb_k_prompt, sb_v_prompt,
            fox_k_sample, fox_v_sample, fox_logf_sample, sb_k_sample, sb_v_sample)
```

```python
import functools
import math

import jax
import jax.numpy as jnp
from jax import lax
from jax.experimental import pallas as pl
from jax.experimental.pallas import tpu as pltpu

F32 = jnp.float32
BF16 = jnp.bfloat16
HI = lax.Precision.HIGHEST

HEAD_DIM = 128
N_FOX = 8
N_SB = 8
TOP_K = 4
N_EXPERTS = 32
LANES = 128
NORM_EPS = 1e-6
SWIGLU_ALPHA = 1.702
SWIGLU_LIMIT = 7.0
ATTN_SCALE = HEAD_DIM ** -0.5
LOG2E = math.log2(math.e)
VMEM_LIMIT = 56 * 1024 * 1024
NEG_BIG = -1e30


def _cp(sem, vmem=VMEM_LIMIT):
    return pltpu.CompilerParams(dimension_semantics=sem, vmem_limit_bytes=vmem)


def _rms(x, g):
    return x * lax.rsqrt(jnp.mean(x * x, axis=-1, keepdims=True) + NORM_EPS) * g


def _log_sigmoid(x):
    return jnp.minimum(x, 0.0) - jnp.log(1.0 + jnp.exp(-jnp.abs(x)))


def _softplus(x):
    return jnp.maximum(x, 0.0) + jnp.log(1.0 + jnp.exp(-jnp.abs(x)))


def _dot(a, b, precision=None):
    return jnp.dot(a, b, preferred_element_type=F32, precision=precision)


def _dot_nt(a, b):
    return lax.dot_general(a, b, (((1,), (1,)), ((), ())), preferred_element_type=F32)


def _ada_kernel(c_ref, w_ref, b_ref, o_ref):
    c = c_ref[...]
    s = c / (1.0 + jnp.exp(-c))
    o_ref[...] = _dot(s.astype(BF16), w_ref[...].astype(BF16)) + b_ref[...]


def _ada(c_all, w, b, tn=1024):
    r, d = c_all.shape
    n = w.shape[1]
    return pl.pallas_call(
        _ada_kernel,
        grid=(n // tn,),
        in_specs=[pl.BlockSpec((r, d), lambda j: (0, 0)),
                  pl.BlockSpec((d, tn), lambda j: (0, j)),
                  pl.BlockSpec((1, tn), lambda j: (0, j))],
        out_specs=pl.BlockSpec((r, tn), lambda j: (0, j)),
        out_shape=jax.ShapeDtypeStruct((r, n), F32),
        compiler_params=_cp(("arbitrary",)),
        name="ada_mod",
    )(c_all, w, b)


def _inproj_kernel(x_ref, sc_ref, sh_ref, g_ref, w_ref, wf_ref, bf_ref,
                   z_ref, zb_ref, lf_ref, fc_ref, h_scr, carry_scr):
    i = pl.program_id(0)
    j = pl.program_id(1)
    tm = x_ref.shape[0]

    @pl.when(j == 0)
    def _():
        h = _rms(x_ref[...], g_ref[...]) * (1.0 + sc_ref[...]) + sh_ref[...]
        hb = h.astype(BF16)
        h_scr[...] = hb
        lf = _log_sigmoid(_dot(hb, wf_ref[...]) + bf_ref[...])
        lf_ref[...] = lf

        @pl.when(i == 0)
        def _():
            carry_scr[...] = jnp.zeros_like(carry_scr)

        rows = lax.broadcasted_iota(jnp.int32, (tm, tm), 0)
        cols = lax.broadcasted_iota(jnp.int32, (tm, tm), 1)
        tri = (cols <= rows).astype(F32)
        csum = _dot(tri, lf, HI) + carry_scr[...]
        fc_ref[...] = csum
        carry_scr[...] = csum[tm - 1:tm, :]

    z = _dot(h_scr[...], w_ref[...])
    z_ref[...] = z
    zb_ref[...] = z.astype(BF16)


def _inproj_prompt(x, sc, sh, g, w6, wf, bfor, tm=512, tn=1024):
    p, d = x.shape
    n = w6.shape[1]
    row = lambda i, j: (i, 0)
    const = lambda i, j: (0, 0)
    return pl.pallas_call(
        _inproj_kernel,
        grid=(p // tm, n // tn),
        in_specs=[pl.BlockSpec((tm, d), row),
                  pl.BlockSpec((1, d), const), pl.BlockSpec((1, d), const), pl.BlockSpec((1, d), const),
                  pl.BlockSpec((d, tn), lambda i, j: (0, j)),
                  pl.BlockSpec((d, LANES), const), pl.BlockSpec((1, LANES), const)],
        out_specs=[pl.BlockSpec((tm, tn), lambda i, j: (i, j)),
                   pl.BlockSpec((tm, tn), lambda i, j: (i, j)),
                   pl.BlockSpec((tm, LANES), row),
                   pl.BlockSpec((tm, LANES), row)],
        out_shape=[jax.ShapeDtypeStruct((p, n), F32), jax.ShapeDtypeStruct((p, n), BF16),
                   jax.ShapeDtypeStruct((p, LANES), F32), jax.ShapeDtypeStruct((p, LANES), F32)],
        scratch_shapes=[pltpu.VMEM((tm, d), BF16), pltpu.VMEM((1, LANES), F32)],
        compiler_params=_cp(("arbitrary", "arbitrary")),
        name="inproj_prompt",
    )(x, sc, sh, g, w6, wf, bfor)


def _fox_kernel(q_ref, k_ref, v_ref, nf_ref, g_ref, o_ref, m_scr, l_scr, acc_scr, *, tq, tk):
    i = pl.program_id(1)
    q = (q_ref[...].astype(F32) * (ATTN_SCALE * LOG2E)).astype(BF16)
    m_scr[...] = jnp.full_like(m_scr, -jnp.inf)
    l_scr[...] = jnp.zeros_like(l_scr)
    acc_scr[...] = jnp.zeros_like(acc_scr)

    def step(j, masked):
        k = k_ref[pl.ds(pl.multiple_of(j * tk, tk), tk), :]
        v = v_ref[pl.ds(pl.multiple_of(j * tk, tk), tk), :]
        s = _dot_nt(q, k) + nf_ref[j]
        if masked:
            rows = i * tq + lax.broadcasted_iota(jnp.int32, (tq, tk), 0)
            cols = j * tk + lax.broadcasted_iota(jnp.int32, (tq, tk), 1)
            s = jnp.where(cols <= rows, s, -jnp.inf)
        m_prev = m_scr[...]
        m_new = jnp.maximum(m_prev, jnp.max(s, axis=1, keepdims=True))
        alpha = jnp.exp2(m_prev - m_new)
        p = jnp.exp2(s - m_new)
        l_scr[...] = alpha * l_scr[...] + jnp.sum(p, axis=1, keepdims=True)
        acc_scr[...] = alpha * acc_scr[...] + _dot(p.astype(BF16), v)
        m_scr[...] = m_new

    n_full = (i * tq) // tk

    def body(j, c):
        step(j, False)
        return c

    lax.fori_loop(0, n_full, body, 0)
    for dgn in range(tq // tk):
        step(n_full + dgn, True)
    o = acc_scr[...] / l_scr[...]
    o_ref[...] = _rms(o, g_ref[...]).astype(o_ref.dtype)


def _fox_attn(zb, nf2, g, col_q, col_k, col_v, tq=512, tk=512):
    p = zb.shape[0]
    d = HEAD_DIM
    return pl.pallas_call(
        functools.partial(_fox_kernel, tq=tq, tk=tk),
        grid=(N_FOX, p // tq),
        in_specs=[pl.BlockSpec((tq, d), lambda h, i: (i, col_q + h)),
                  pl.BlockSpec((p, d), lambda h, i: (0, col_k + h)),
                  pl.BlockSpec((p, d), lambda h, i: (0, col_v + h)),
                  pl.BlockSpec((None, p // tk, 1, tk), lambda h, i: (h, 0, 0, 0)),
                  pl.BlockSpec((None, 1, d), lambda h, i: (h, 0, 0))],
        out_specs=pl.BlockSpec((tq, d), lambda h, i: (i, h)),
        out_shape=jax.ShapeDtypeStruct((p, N_FOX * d), BF16),
        scratch_shapes=[pltpu.VMEM((tq, 1), F32), pltpu.VMEM((tq, 1), F32), pltpu.VMEM((tq, d), F32)],
        compiler_params=_cp(("arbitrary", "arbitrary")),
        name="fox_attn",
    )(zb, zb, zb, nf2, g)


SB_CHUNK = 2 * LANES


def _sb_kernel(q_ref, k_ref, v_ref, b_ref, g_ref, to_ref, o_ref, carry_scr, acc_scr, *, tq):
    i = pl.program_id(1)
    tk = SB_CHUNK
    q = (q_ref[...].astype(F32) * ATTN_SCALE).astype(BF16)
    carry_scr[...] = jnp.zeros_like(carry_scr)
    acc_scr[...] = jnp.zeros_like(acc_scr)

    def step(j, masked):
        k = k_ref[pl.ds(pl.multiple_of(j * tk, tk), tk), :]
        v = v_ref[pl.ds(pl.multiple_of(j * tk, tk), tk), :]
        z = _dot_nt(q, k) + b_ref[...]
        sp = _softplus(z)
        if masked:
            rows = i * tq + lax.broadcasted_iota(jnp.int32, (tq, tk), 0)
            cols = j * tk + lax.broadcasted_iota(jnp.int32, (tq, tk), 1)
            valid = cols < rows
            sp = jnp.where(valid, sp, 0.0)
        carry = carry_scr[...]
        parts = [None, None]
        for half in (1, 0):
            lo_c, hi_c = half * LANES, (half + 1) * LANES
            sph = sp[:, lo_c:hi_c]
            hi = sph.astype(BF16)
            lo = (sph - hi.astype(F32)).astype(BF16)
            r = _dot(jnp.concatenate([hi, lo], axis=1), to_ref[...])
            a = jnp.exp(z[:, lo_c:hi_c] - r[:, :LANES] - carry)
            if masked:
                a = jnp.where(valid[:, lo_c:hi_c], a, 0.0)
            parts[half] = a.astype(BF16)
            carry = carry + r[:, LANES:]
        carry_scr[...] = carry
        acc_scr[...] += _dot(jnp.concatenate(parts, axis=1), v)

    n_full = (i * tq) // tk
    for dgn in reversed(range(tq // tk)):
        step(n_full + dgn, True)

    def body(jj, c):
        step(n_full - 1 - jj, False)
        return c

    lax.fori_loop(0, n_full, body, 0)
    o_ref[...] = _rms(acc_scr[...], g_ref[...]).astype(o_ref.dtype)


def _sb_tri_ones():
    r = lax.broadcasted_iota(jnp.int32, (LANES, LANES), 0)
    c = lax.broadcasted_iota(jnp.int32, (LANES, LANES), 1)
    blk = jnp.concatenate([(r >= c).astype(BF16), jnp.ones((LANES, LANES), BF16)], axis=1)
    return jnp.concatenate([blk, blk], axis=0)


def _sb_attn(zb, bias, g, col_q, col_k, col_v, tq=512):
    p = zb.shape[0]
    d = HEAD_DIM
    return pl.pallas_call(
        functools.partial(_sb_kernel, tq=tq),
        grid=(N_SB, p // tq),
        in_specs=[pl.BlockSpec((tq, d), lambda h, i: (i, col_q + h)),
                  pl.BlockSpec((p, d), lambda h, i: (0, col_k + h)),
                  pl.BlockSpec((p, d), lambda h, i: (0, col_v + h)),
                  pl.BlockSpec((None, 1, SB_CHUNK), lambda h, i: (h, 0, 0)),
                  pl.BlockSpec((None, 1, d), lambda h, i: (h, 0, 0)),
                  pl.BlockSpec((SB_CHUNK, SB_CHUNK), lambda h, i: (0, 0))],
        out_specs=pl.BlockSpec((tq, d), lambda h, i: (i, h)),
        out_shape=jax.ShapeDtypeStruct((p, N_SB * d), BF16),
        scratch_shapes=[pltpu.VMEM((tq, LANES), F32), pltpu.VMEM((tq, d), F32)],
        compiler_params=_cp(("arbitrary", "arbitrary")),
        name="sb_attn",
    )(zb, zb, zb, bias, g, _sb_tri_ones())


def _mixout_kernel(of_ref, os_ref, x_ref, gt_ref, sc_ref, sh_ref, gpm_ref, gpf_ref, wo_ref, wr_ref, br_ref,
                   x1_ref, h2_ref, idx_ref, gate_ref, rank_ref, wd_ref, cnt_ref, carry_scr):
    i = pl.program_id(0)
    tm = x_ref.shape[0]
    half = of_ref.shape[1]

    @pl.when(i == 0)
    def _():
        carry_scr[...] = jnp.zeros_like(carry_scr)

    mix = (_dot(of_ref[...].astype(BF16), wo_ref[0:half, :])
           + _dot(os_ref[...].astype(BF16), wo_ref[half:2 * half, :]))
    x1 = x_ref[...] + gt_ref[...] * _rms(mix, gpm_ref[...])
    x1_ref[...] = x1
    h2 = _rms(x1, gpf_ref[...]) * (1.0 + sc_ref[...]) + sh_ref[...]
    h2_ref[...] = h2
    logits = _dot(h2.astype(BF16), wr_ref[...]) + br_ref[...]
    lane = lax.broadcasted_iota(jnp.int32, (tm, LANES), 1)
    lane_f = lane.astype(F32)
    vals = logits
    top_i, top_v = [], []
    for _ in range(TOP_K):
        mk = jnp.max(vals, axis=1, keepdims=True)
        ik = jnp.min(jnp.where(vals == mk, lane_f, float(LANES)), axis=1, keepdims=True)
        vals = jnp.where(lane_f == ik, -jnp.inf, vals)
        top_i.append(ik)
        top_v.append(mk)
    ex = [jnp.exp(v - top_v[0]) for v in top_v]
    den = ex[0] + ex[1] + ex[2] + ex[3]
    gates = [e / den for e in ex]
    hot = [lane_f == ik for ik in top_i]
    sel = jnp.zeros((tm, LANES), F32)
    wdense = jnp.zeros((tm, LANES), F32)
    for k in range(TOP_K):
        sel = sel + hot[k].astype(F32)
        wdense = wdense + jnp.where(hot[k], gates[k], 0.0)
    rows = lax.broadcasted_iota(jnp.int32, (tm, tm), 0)
    cols = lax.broadcasted_iota(jnp.int32, (tm, tm), 1)
    before = (cols < rows).astype(BF16)
    cnt = _dot(before, sel.astype(BF16)) + carry_scr[...]
    idx_o = jnp.zeros((tm, LANES), F32)
    gate_o = jnp.zeros((tm, LANES), F32)
    rank_o = jnp.zeros((tm, LANES), F32)
    for k in range(TOP_K):
        rk = jnp.sum(jnp.where(hot[k], cnt, 0.0), axis=1, keepdims=True)
        idx_o = jnp.where(lane == k, top_i[k], idx_o)
        gate_o = jnp.where(lane == k, gates[k], gate_o)
        rank_o = jnp.where(lane == k, rk, rank_o)
    idx_ref[...] = idx_o.astype(jnp.int32)
    gate_ref[...] = gate_o
    rank_ref[...] = rank_o.astype(jnp.int32)
    wd_ref[...] = wdense
    total = carry_scr[...] + jnp.sum(sel, axis=0, keepdims=True)
    carry_scr[...] = total
    cnt_ref[...] = total


def _mixout(of, os_, x, gt, sc, sh, gpm, gpf, wo, wr, br, tm):
    p, d = x.shape
    half = of.shape[1]
    row = lambda i: (i, 0)
    const = lambda i: (0, 0)
    mrow = row if gt.shape[0] == p else const
    msh = (tm, d) if gt.shape[0] == p else (1, d)
    return pl.pallas_call(
        _mixout_kernel,
        grid=(p // tm,),
        in_specs=[pl.BlockSpec((tm, half), row), pl.BlockSpec((tm, half), row), pl.BlockSpec((tm, d), row),
                  pl.BlockSpec(msh, mrow), pl.BlockSpec(msh, mrow), pl.BlockSpec(msh, mrow),
                  pl.BlockSpec((1, d), const), pl.BlockSpec((1, d), const),
                  pl.BlockSpec((2 * half, d), const),
                  pl.BlockSpec((d, LANES), const), pl.BlockSpec((1, LANES), const)],
        out_specs=[pl.BlockSpec((tm, d), row), pl.BlockSpec((tm, d), row),
                   pl.BlockSpec((tm, LANES), row), pl.BlockSpec((tm, LANES), row),
                   pl.BlockSpec((tm, LANES), row), pl.BlockSpec((tm, LANES), row),
                   pl.BlockSpec((1, LANES), const)],
        out_shape=[jax.ShapeDtypeStruct((p, d), F32), jax.ShapeDtypeStruct((p, d), F32),
                   jax.ShapeDtypeStruct((p, LANES), jnp.int32), jax.ShapeDtypeStruct((p, LANES), F32),
                   jax.ShapeDtypeStruct((p, LANES), jnp.int32), jax.ShapeDtypeStruct((p, LANES), F32),
                   jax.ShapeDtypeStruct((1, LANES), F32)],
        scratch_shapes=[pltpu.VMEM((1, LANES), F32)],
        compiler_params=_cp(("arbitrary",)),
        name="mixout_router",
    )(of, os_, x, gt, sc, sh, gpm, gpf, wo, wr, br)


def _dispatch_kernel(dest_ref, h_hbm, xs_in, xs_out, sem, *, tt):
    del xs_in
    i = pl.program_id(0)

    def copy(t, k):
        d = dest_ref[t * TOP_K + k]
        return pltpu.make_async_copy(h_hbm.at[pl.ds(i * tt + t, 1)], xs_out.at[pl.ds(d, 1)], sem)

    def start(t, c):
        for k in range(TOP_K):
            copy(t, k).start()
        return c

    def wait(t, c):
        for k in range(TOP_K):
            copy(t, k).wait()
        return c

    lax.fori_loop(0, tt, start, 0)
    lax.fori_loop(0, tt, wait, 0)


def _dispatch(dest_flat, h2, xs_zero, tt=128):
    p, d = h2.shape
    return pl.pallas_call(
        functools.partial(_dispatch_kernel, tt=tt),
        grid=(p // tt,),
        in_specs=[pl.BlockSpec((tt * TOP_K,), lambda i: (i,), memory_space=pltpu.SMEM),
                  pl.BlockSpec(memory_space=pl.ANY),
                  pl.BlockSpec(memory_space=pl.ANY)],
        out_specs=pl.BlockSpec(memory_space=pl.ANY),
        out_shape=jax.ShapeDtypeStruct(xs_zero.shape, xs_zero.dtype),
        scratch_shapes=[pltpu.SemaphoreType.DMA(())],
        input_output_aliases={2: 0},
        compiler_params=_cp(("arbitrary",)),
        name="moe_dispatch",
    )(dest_flat, h2, xs_zero)


def _gate_up_kernel(te_ref, nu_ref, x_ref, wg_ref, wu_ref, bg_ref, bu_ref, h_ref, wgb, wub):
    i = pl.program_id(1)

    @pl.when(i < nu_ref[0])
    def _():
        prev = te_ref[jnp.maximum(i - 1, 0)]
        fresh = jnp.logical_or(i == 0, te_ref[i] != prev)

        @pl.when(fresh)
        def _():
            wgb[...] = wg_ref[...].astype(BF16)
            wub[...] = wu_ref[...].astype(BF16)

        x = x_ref[...].astype(BF16)
        g = jnp.minimum(_dot(x, wgb[...]) + bg_ref[...], SWIGLU_LIMIT)
        u = jnp.clip(_dot(x, wub[...]) + bu_ref[...], -SWIGLU_LIMIT, SWIGLU_LIMIT)
        act = g * (1.0 / (1.0 + jnp.exp(-SWIGLU_ALPHA * g))) * (u + 1.0)
        h_ref[...] = act.astype(h_ref.dtype)

    @pl.when(i >= nu_ref[0])
    def _():
        h_ref[...] = jnp.zeros_like(h_ref)


def _gate_up(tile_e, n_used, xs, w_gu, b_gu, tm, tn=512):
    r, d = xs.shape
    f = w_gu.shape[2] // 2
    nb = f // tn

    def tile(n, i, te, nu):
        return jnp.minimum(i, nu[0] - 1)

    return pl.pallas_call(
        _gate_up_kernel,
        grid_spec=pltpu.PrefetchScalarGridSpec(
            num_scalar_prefetch=2,
            grid=(nb, r // tm),
            in_specs=[pl.BlockSpec((tm, d), lambda n, i, te, nu: (tile(n, i, te, nu), 0)),
                      pl.BlockSpec((None, d, tn), lambda n, i, te, nu: (te[tile(n, i, te, nu)], 0, n)),
                      pl.BlockSpec((None, d, tn), lambda n, i, te, nu: (te[tile(n, i, te, nu)], 0, nb + n)),
                      pl.BlockSpec((None, 1, tn), lambda n, i, te, nu: (te[tile(n, i, te, nu)], 0, n)),
                      pl.BlockSpec((None, 1, tn), lambda n, i, te, nu: (te[tile(n, i, te, nu)], 0, nb + n))],
            out_specs=pl.BlockSpec((tm, tn), lambda n, i, te, nu: (i, n)),
            scratch_shapes=[pltpu.VMEM((d, tn), BF16), pltpu.VMEM((d, tn), BF16)]),
        out_shape=jax.ShapeDtypeStruct((r, f), BF16),
        compiler_params=_cp(("arbitrary", "arbitrary")),
        name="moe_gate_up",
    )(tile_e, n_used, xs, w_gu, w_gu, b_gu, b_gu)


def _down_kernel(te_ref, nu_ref, h_ref, w_ref, b_ref, y_ref, wb):
    i = pl.program_id(1)

    @pl.when(i < nu_ref[0])
    def _():
        prev = te_ref[jnp.maximum(i - 1, 0)]
        fresh = jnp.logical_or(i == 0, te_ref[i] != prev)

        @pl.when(fresh)
        def _():
            wb[...] = w_ref[...].astype(BF16)

        y_ref[...] = _dot(h_ref[...], wb[...]) + b_ref[...]

    @pl.when(i >= nu_ref[0])
    def _():
        y_ref[...] = jnp.zeros_like(y_ref)


def _down(tile_e, n_used, hs, w_dn, b_dn, tm, tn=1024):
    r, f = hs.shape
    d = w_dn.shape[2]

    def tile(n, i, te, nu):
        return jnp.minimum(i, nu[0] - 1)

    return pl.pallas_call(
        _down_kernel,
        grid_spec=pltpu.PrefetchScalarGridSpec(
            num_scalar_prefetch=2,
            grid=(d // tn, r // tm),
            in_specs=[pl.BlockSpec((tm, f), lambda n, i, te, nu: (tile(n, i, te, nu), 0)),
                      pl.BlockSpec((None, f, tn), lambda n, i, te, nu: (te[tile(n, i, te, nu)], 0, n)),
                      pl.BlockSpec((None, 1, tn), lambda n, i, te, nu: (te[tile(n, i, te, nu)], 0, n))],
            out_specs=pl.BlockSpec((tm, tn), lambda n, i, te, nu: (i, n)),
            scratch_shapes=[pltpu.VMEM((f, tn), BF16)]),
        out_shape=jax.ShapeDtypeStruct((r, d), F32),
        compiler_params=_cp(("arbitrary", "arbitrary")),
        name="moe_down",
    )(tile_e, n_used, hs, w_dn, b_dn)


def _combine_kernel(dest_ref, y_hbm, gate_ref, x1_ref, gt_ref, g_ref, o_ref, buf, sem, *, tt):
    def copy(t, k):
        d = dest_ref[t * TOP_K + k]
        return pltpu.make_async_copy(y_hbm.at[pl.ds(d, 1)], buf.at[k, pl.ds(t, 1)], sem)

    def start(t, c):
        for k in range(TOP_K):
            copy(t, k).start()
        return c

    def wait(t, c):
        for k in range(TOP_K):
            copy(t, k).wait()
        return c

    lax.fori_loop(0, tt, start, 0)
    lax.fori_loop(0, tt, wait, 0)
    gate = gate_ref[...]
    ffn = gate[:, 0:1] * buf[0]
    for k in range(1, TOP_K):
        ffn = ffn + gate[:, k:k + 1] * buf[k]
    o_ref[...] = x1_ref[...] + gt_ref[...] * _rms(ffn, g_ref[...])


def _combine(dest_flat, ys, gates, x1, gt, g, tt=128):
    p, d = x1.shape
    row = lambda i: (i, 0)
    const = lambda i: (0, 0)
    return pl.pallas_call(
        functools.partial(_combine_kernel, tt=tt),
        grid=(p // tt,),
        in_specs=[pl.BlockSpec((tt * TOP_K,), lambda i: (i,), memory_space=pltpu.SMEM),
                  pl.BlockSpec(memory_space=pl.ANY),
                  pl.BlockSpec((tt, LANES), row), pl.BlockSpec((tt, d), row),
                  pl.BlockSpec((1, d), const), pl.BlockSpec((1, d), const)],
        out_specs=pl.BlockSpec((tt, d), row),
        out_shape=jax.ShapeDtypeStruct((p, d), F32),
        scratch_shapes=[pltpu.VMEM((TOP_K, tt, d), F32), pltpu.SemaphoreType.DMA(())],
        compiler_params=_cp(("arbitrary",)),
        name="moe_combine",
    )(dest_flat, ys, gates, x1, gt, g)


def _moe_prompt(h2, top_idx, gates, rank, counts, x1, gt_m, g_post, w_gu, b_gu, w_dn, b_dn, tm=512):
    p, d = h2.shape
    n_tiles = (p * TOP_K) // tm + N_EXPERTS
    padded = (counts + tm - 1) // tm * tm
    pends = jnp.cumsum(padded)
    pstart = pends - padded
    dest = (pstart[top_idx] + rank).reshape(-1).astype(jnp.int32)
    bstart = jnp.arange(n_tiles, dtype=jnp.int32) * tm
    tile_e = jnp.minimum(jnp.sum(pends[None, :] <= bstart[:, None], axis=1), N_EXPERTS - 1).astype(jnp.int32)
    n_used = (pends[-1:] // tm).astype(jnp.int32)
    xs = _dispatch(dest, h2, jnp.zeros((n_tiles * tm, d), F32))
    hs = _gate_up(tile_e, n_used, xs, w_gu, b_gu[:, None, :], tm)
    ys = _down(tile_e, n_used, hs, w_dn, b_dn[:, None, :], tm)
    return _combine(dest, ys, gates, x1, gt_m, g_post)


def _inproj_sample_kernel(x_ref, sc_ref, sh_ref, g_ref, w_ref, bf_ref, z_ref, lf_ref, h_scr, *, nk, tk, off_f):
    k = pl.program_id(0)

    @pl.when(k == 0)
    def _():
        h = _rms(x_ref[...], g_ref[...]) * (1.0 + sc_ref[...]) + sh_ref[...]
        for kk in range(nk):
            h_scr[kk] = h[:, kk * tk:(kk + 1) * tk]
        z_ref[...] = jnp.zeros_like(z_ref)

    z_ref[...] += _dot(h_scr[k].astype(BF16), w_ref[...].astype(BF16))

    @pl.when(k == nk - 1)
    def _():
        lf_ref[...] = _log_sigmoid(z_ref[:, off_f:off_f + LANES] + bf_ref[...])


def _inproj_sample(x, sc, sh, g, w, bfor, off_f, tk=256):
    n, d = x.shape
    dn = w.shape[1]
    nk = d // tk
    const = lambda k: (0, 0)
    return pl.pallas_call(
        functools.partial(_inproj_sample_kernel, nk=nk, tk=tk, off_f=off_f),
        grid=(nk,),
        in_specs=[pl.BlockSpec((n, d), const), pl.BlockSpec((n, d), const), pl.BlockSpec((n, d), const),
                  pl.BlockSpec((1, d), const), pl.BlockSpec((tk, dn), lambda k: (k, 0)),
                  pl.BlockSpec((1, LANES), const)],
        out_specs=[pl.BlockSpec((n, dn), const), pl.BlockSpec((n, LANES), const)],
        out_shape=[jax.ShapeDtypeStruct((n, dn), F32), jax.ShapeDtypeStruct((n, LANES), F32)],
        scratch_shapes=[pltpu.VMEM((nk, n, tk), F32)],
        compiler_params=_cp(("arbitrary",)),
        name="inproj_sample",
    )(x, sc, sh, g, w, bfor)


def _decode_kernel(pt_ref, fq_ref, fkn_ref, fvn_ref, lfn_ref, sq_ref, bsb_ref, gf_ref, gs_ref,
                   fk_ref, fv_ref, lf_ref, sk_ref, sv_ref, of_ref, os_ref,
                   m_scr, l_scr, acc_scr, dsuf_scr, sacc_scr, skeep_scr, *, n_pages):
    del pt_ref
    p = pl.program_id(1)
    page = fk_ref.shape[0]
    nh = fk_ref.shape[1]
    sub = lax.broadcasted_iota(jnp.int32, (nh, LANES), 0)
    q = fq_ref[...].astype(BF16)

    @pl.when(p == 0)
    def _():
        kn = fkn_ref[...].astype(BF16).astype(F32)
        m_scr[...] = jnp.sum(q.astype(F32) * kn, axis=1, keepdims=True) * ATTN_SCALE
        l_scr[...] = jnp.ones_like(l_scr)
        acc_scr[...] = fvn_ref[...].astype(BF16).astype(F32)
        dsuf_scr[...] = jnp.zeros_like(dsuf_scr)
        sacc_scr[...] = jnp.zeros_like(sacc_scr)
        skeep_scr[...] = jnp.zeros_like(skeep_scr)

    def scores(qv, k_ref):
        qb = qv
        s = jnp.zeros((nh, page), F32)
        for h in range(nh):
            s = jnp.where(sub == h, _dot_nt(qb, k_ref[:, h, :].astype(BF16)), s)
        return s

    def weighted(w, v_ref):
        wb = w.astype(BF16)
        o = jnp.zeros((nh, HEAD_DIM), F32)
        for h in range(nh):
            o = jnp.where(sub == h, _dot(wb, v_ref[:, h, :].astype(BF16)), o)
        return o

    rr = lax.broadcasted_iota(jnp.int32, (page, page), 0)
    cc = lax.broadcasted_iota(jnp.int32, (page, page), 1)
    later = (rr > cc).astype(F32)

    lf = lf_ref[...]
    suf = _dot(lf, later, HI) + dsuf_scr[...]
    sc = scores(q, fk_ref) * ATTN_SCALE + lfn_ref[...] + suf
    m_prev = m_scr[...]
    m_new = jnp.maximum(m_prev, jnp.max(sc, axis=1, keepdims=True))
    corr = jnp.exp(m_prev - m_new)
    pr = jnp.exp(sc - m_new)
    l_scr[...] = l_scr[...] * corr + jnp.sum(pr, axis=1, keepdims=True)
    acc_scr[...] = acc_scr[...] * corr + weighted(pr, fv_ref)
    m_scr[...] = m_new
    dsuf_scr[...] += jnp.sum(lf, axis=1, keepdims=True)

    z = scores(sq_ref[...].astype(BF16), sk_ref) * ATTN_SCALE + bsb_ref[...]
    sp = _softplus(z)
    a = jnp.exp(z - sp - _dot(sp, later, HI) - skeep_scr[...])
    sacc_scr[...] += weighted(a, sv_ref)
    skeep_scr[...] += jnp.sum(sp, axis=1, keepdims=True)

    @pl.when(p == n_pages - 1)
    def _():
        of_ref[...] = _rms(acc_scr[...] / l_scr[...], gf_ref[...])
        os_ref[...] = _rms(sacc_scr[...], gs_ref[...])


def _decode(page_table, fq, fkn, fvn, lfn, sq, bsb, gf, gs, ck, cv, clf_t, csk, csv):
    n, n_pages = page_table.shape
    nh, d = fq.shape[1], fq.shape[2]
    page = ck.shape[1]
    per_seq = lambda i, p, pt: (i, 0, 0)
    const = lambda i, p, pt: (0, 0)
    paged4 = lambda i, p, pt: (pt[i, n_pages - 1 - p], 0, 0, 0)
    paged3 = lambda i, p, pt: (pt[i, n_pages - 1 - p], 0, 0)
    vec = pl.BlockSpec((None, nh, d), per_seq)
    col = pl.BlockSpec((None, nh, 1), per_seq)
    return pl.pallas_call(
        functools.partial(_decode_kernel, n_pages=n_pages),
        grid_spec=pltpu.PrefetchScalarGridSpec(
            num_scalar_prefetch=1,
            grid=(n, n_pages),
            in_specs=[vec, vec, vec, col, vec,
                      pl.BlockSpec((nh, 1), const), pl.BlockSpec((nh, d), const), pl.BlockSpec((nh, d), const),
                      pl.BlockSpec((None, page, nh, d), paged4), pl.BlockSpec((None, page, nh, d), paged4),
                      pl.BlockSpec((None, nh, page), paged3),
                      pl.BlockSpec((None, page, nh, d), paged4), pl.BlockSpec((None, page, nh, d), paged4)],
            out_specs=[vec, vec],
            scratch_shapes=[pltpu.VMEM((nh, 1), F32), pltpu.VMEM((nh, 1), F32), pltpu.VMEM((nh, d), F32),
                            pltpu.VMEM((nh, 1), F32), pltpu.VMEM((nh, d), F32), pltpu.VMEM((nh, 1), F32)]),
        out_shape=[jax.ShapeDtypeStruct((n, nh, d), F32), jax.ShapeDtypeStruct((n, nh, d), F32)],
        compiler_params=_cp(("arbitrary", "arbitrary")),
        name="decode_attn",
    )(page_table, fq, fkn, fvn, lfn, sq, bsb, gf, gs, ck, cv, clf_t, csk, csv)


def _moe_small_kernel(h_ref, wd_ref, wg_ref, wu_ref, bg_ref, bu_ref, wdn_ref, bd_ref, x1_ref, gt_ref, g_ref,
                      y_ref, acc_scr, *, n_e, n_j):
    e = pl.program_id(0)
    j = pl.program_id(1)
    n = h_ref.shape[0]

    @pl.when(jnp.logical_and(e == 0, j == 0))
    def _():
        acc_scr[...] = jnp.zeros_like(acc_scr)

    lane = lax.broadcasted_iota(jnp.int32, (n, LANES), 1)
    we = jnp.sum(jnp.where(lane == e, wd_ref[...], 0.0), axis=1, keepdims=True)
    hb = h_ref[...].astype(BF16)
    g = jnp.minimum(_dot(hb, wg_ref[...].astype(BF16)) + bg_ref[...], SWIGLU_LIMIT)
    u = jnp.clip(_dot(hb, wu_ref[...].astype(BF16)) + bu_ref[...], -SWIGLU_LIMIT, SWIGLU_LIMIT)
    act = g * (1.0 / (1.0 + jnp.exp(-SWIGLU_ALPHA * g))) * (u + 1.0)
    y = _dot(act.astype(BF16), wdn_ref[...].astype(BF16))
    acc_scr[...] += we * y

    @pl.when(j == 0)
    def _():
        acc_scr[...] += we * bd_ref[...]

    @pl.when(jnp.logical_and(e == n_e - 1, j == n_j - 1))
    def _():
        y_ref[...] = x1_ref[...] + gt_ref[...] * _rms(acc_scr[...], g_ref[...])


def _moe_small(h2, wdense, w_gu, b_gu, w_dn, b_dn, x1, gt, g, tf=512):
    n, d = h2.shape
    n_e = w_gu.shape[0]
    f = w_gu.shape[2] // 2
    nj = f // tf
    const = lambda e, j: (0, 0)
    return pl.pallas_call(
        functools.partial(_moe_small_kernel, n_e=n_e, n_j=nj),
        grid=(n_e, nj),
        in_specs=[pl.BlockSpec((n, d), const), pl.BlockSpec((n, LANES), const),
                  pl.BlockSpec((None, d, tf), lambda e, j: (e, 0, j)),
                  pl.BlockSpec((None, d, tf), lambda e, j: (e, 0, nj + j)),
                  pl.BlockSpec((None, 1, tf), lambda e, j: (e, 0, j)),
                  pl.BlockSpec((None, 1, tf), lambda e, j: (e, 0, nj + j)),
                  pl.BlockSpec((None, tf, d), lambda e, j: (e, j, 0)),
                  pl.BlockSpec((None, 1, d), lambda e, j: (e, 0, 0)),
                  pl.BlockSpec((n, d), const), pl.BlockSpec((n, d), const), pl.BlockSpec((1, d), const)],
        out_specs=pl.BlockSpec((n, d), const),
        out_shape=jax.ShapeDtypeStruct((n, d), F32),
        scratch_shapes=[pltpu.VMEM((n, d), F32)],
        compiler_params=_cp(("arbitrary", "arbitrary")),
        name="moe_sample",
    )(h2, wdense, w_gu, w_gu, b_gu, b_gu, w_dn, b_dn, x1, gt, g)


def _pad_lanes(a, fill=0.0):
    return jnp.pad(a, ((0, 0), (0, LANES - a.shape[1])), constant_values=fill)


def kernel(x_prompt, x_sample, cache_fox_k, cache_fox_v, cache_fox_logf, cache_sb_k, cache_sb_v, page_table,
           c_prompt, c_sample, w_ada, b_ada, g_pre_mix, w_in, b_forget, b_sb, g_fox_out, g_sb_out, w_out,
           g_post_mix, g_pre_ffn, w_router, b_router, w_gate_up, b_gate_up, w_down, b_down, g_post_ffn):
    depth = w_ada.shape[0]
    assert depth == 1 and x_sample.shape[1] == 1 and x_prompt.shape[0] == 1
    bsz, seq, d = x_prompt.shape
    ns = x_sample.shape[0]
    p = bsz * seq
    dfox = N_FOX * HEAD_DIM
    dsb = N_SB * HEAD_DIM
    off_ff = 3 * dfox
    off_sq = off_ff + N_FOX
    xp = x_prompt.reshape(p, d)
    xs = x_sample.reshape(ns, d)

    n_c = bsz + ns
    c_all = jnp.concatenate([c_prompt, c_sample, jnp.zeros((-n_c % 8, d), F32)], axis=0)
    mod = _ada(c_all, w_ada[0], b_ada[0][None])
    sh_a, sc_a, gt_a, sh_m, sc_m, gt_m = [mod[0:bsz, k * d:(k + 1) * d] for k in range(6)]
    ssh_a, ssc_a, sgt_a, ssh_m, ssc_m, sgt_m = [mod[bsz:n_c, k * d:(k + 1) * d] for k in range(6)]

    w_in0 = w_in[0]
    bfor = _pad_lanes(b_forget[0][None])
    gpre = g_pre_mix[0][None]
    gf = g_fox_out[0]
    gs = g_sb_out[0]
    wr = _pad_lanes(w_router[0]).astype(BF16)
    wo = w_out[0].astype(BF16)
    br = _pad_lanes(b_router[0][None], NEG_BIG)
    gpm = g_post_mix[0][None]
    gpf = g_pre_ffn[0][None]
    gpo = g_post_ffn[0][None]

    w6 = jnp.concatenate([w_in0[:, :off_ff], w_in0[:, off_sq:]], axis=1).astype(BF16)
    wf = _pad_lanes(w_in0[:, off_ff:off_sq]).astype(BF16)
    z, zb, lf, fcum = _inproj_prompt(xp, sc_a, sh_a, gpre, w6, wf, bfor)
    tkf = 512
    nf2 = (-LOG2E * fcum[:, :N_FOX]).T.reshape(N_FOX, p // tkf, 1, tkf)
    hb = dfox // HEAD_DIM
    o_fox = _fox_attn(zb, nf2, gf[:, None, :], 0, hb, 2 * hb, tk=tkf)
    bias_sb = jnp.broadcast_to(b_sb[0][:, None, None], (N_SB, 1, SB_CHUNK))
    o_sb = _sb_attn(zb, bias_sb, gs[:, None, :], 3 * hb, 4 * hb, 5 * hb)
    x1, h2, top_idx, gates, rank, _, counts = _mixout(
        o_fox, o_sb, xp, gt_a, sc_m, sh_m, gpm, gpf, wo, wr, br, tm=256)
    cnt = counts[0, :N_EXPERTS].astype(jnp.int32)
    y_p = _moe_prompt(h2, top_idx[:, :TOP_K], gates, rank[:, :TOP_K], cnt, x1, gt_m, gpo,
                      w_gate_up[0], b_gate_up[0], w_down[0], b_down[0])

    zs, lfs = _inproj_sample(xs, ssc_a, ssh_a, gpre, w_in0, bfor, off_ff)
    heads = lambda a: a.reshape(ns, -1, HEAD_DIM)
    fq_s, fk_s, fv_s = heads(zs[:, :dfox]), heads(zs[:, dfox:2 * dfox]), heads(zs[:, 2 * dfox:off_ff])
    sq_s = heads(zs[:, off_sq:off_sq + dsb])
    sk_s = heads(zs[:, off_sq + dsb:off_sq + 2 * dsb])
    sv_s = heads(zs[:, off_sq + 2 * dsb:])
    lf_s = lfs[:, :N_FOX]
    drop0 = lambda a: a.reshape(a.shape[1:])
    of_s, os_s = _decode(page_table, fq_s, fk_s, fv_s, lf_s[:, :, None], sq_s, b_sb[0][:, None], gf, gs,
                         drop0(cache_fox_k), drop0(cache_fox_v), jnp.swapaxes(drop0(cache_fox_logf), 1, 2),
                         drop0(cache_sb_k), drop0(cache_sb_v))
    x1s, h2s, _, _, _, wdense, _ = _mixout(
        of_s.reshape(ns, dfox), os_s.reshape(ns, dsb), xs, sgt_a, ssc_m, ssh_m, gpm, gpf, wo, wr, br, tm=ns)
    y_s = _moe_small(h2s, wdense, w_gate_up[0], b_gate_up[0][:, None, :], w_down[0], b_down[0][:, None, :],
                     x1s, sgt_m, gpo)

    rows5 = lambda a, nh: a.reshape(1, bsz, seq, nh, HEAD_DIM)
    srows5 = lambda a: a.reshape(1, ns, 1, -1, HEAD_DIM)
    return (y_p.reshape(bsz, seq, d), y_s.reshape(ns, 1, d),
            rows5(z[:, dfox:2 * dfox], N_FOX), rows5(z[:, 2 * dfox:3 * dfox], N_FOX),
            lf[:, :N_FOX].reshape(1, bsz, seq, N_FOX),
            rows5(z[:, 3 * dfox + dsb:3 * dfox + 2 * dsb], N_SB), rows5(z[:, 3 * dfox + 2 * dsb:], N_SB),
            srows5(fk_s), srows5(fv_s), lf_s.reshape(1, ns, 1, N_FOX), srows5(sk_s), srows5(sv_s))
```

```python
import functools
import math

import jax
import jax.numpy as jnp
from jax import lax
from jax.experimental import pallas as pl
from jax.experimental.pallas import tpu as pltpu

F32 = jnp.float32
BF16 = jnp.bfloat16
HI = lax.Precision.HIGHEST

HEAD_DIM = 128
N_FOX = 8
N_SB = 8
TOP_K = 4
N_EXPERTS = 32
LANES = 128
NORM_EPS = 1e-6
SWIGLU_ALPHA = 1.702
SWIGLU_LIMIT = 7.0
ATTN_SCALE = HEAD_DIM ** -0.5
LOG2E = math.log2(math.e)
VMEM_LIMIT = 56 * 1024 * 1024
NEG_BIG = -1e30


def _cp(sem, vmem=VMEM_LIMIT, flags=None):
    return pltpu.CompilerParams(dimension_semantics=sem, vmem_limit_bytes=vmem, flags=flags)


def _rms(x, g):
    return x * lax.rsqrt(jnp.mean(x * x, axis=-1, keepdims=True) + NORM_EPS) * g


def _log_sigmoid(x):
    return jnp.minimum(x, 0.0) - jnp.log(1.0 + jnp.exp(-jnp.abs(x)))


def _softplus(x):
    return jnp.maximum(x, 0.0) + jnp.log(1.0 + jnp.exp(-jnp.abs(x)))


def _dot(a, b, precision=None):
    return jnp.dot(a, b, preferred_element_type=F32, precision=precision)


def _dot_nt(a, b):
    return lax.dot_general(a, b, (((1,), (1,)), ((), ())), preferred_element_type=F32)


def _ada_kernel(c_ref, w_ref, b_ref, o_ref):
    c = c_ref[...]
    s = c / (1.0 + jnp.exp(-c))
    o_ref[...] = _dot(s.astype(BF16), w_ref[...].astype(BF16)) + b_ref[...]


def _ada(c_all, w, b, tn=1024):
    r, d = c_all.shape
    n = w.shape[1]
    return pl.pallas_call(
        _ada_kernel,
        grid=(n // tn,),
        in_specs=[pl.BlockSpec((r, d), lambda j: (0, 0)),
                  pl.BlockSpec((d, tn), lambda j: (0, j)),
                  pl.BlockSpec((1, tn), lambda j: (0, j))],
        out_specs=pl.BlockSpec((r, tn), lambda j: (0, j)),
        out_shape=jax.ShapeDtypeStruct((r, n), F32),
        compiler_params=_cp(("arbitrary",)),
        name="ada_mod",
    )(c_all, w, b)


def _inproj_kernel(x_ref, sc_ref, sh_ref, g_ref, w_ref, wf_ref, bf_ref,
                   z_ref, zb_ref, lf_ref, fc_ref, h_scr, carry_scr):
    i = pl.program_id(0)
    j = pl.program_id(1)
    tm = x_ref.shape[0]

    @pl.when(j == 0)
    def _():
        h = _rms(x_ref[...], g_ref[...]) * (1.0 + sc_ref[...]) + sh_ref[...]
        hb = h.astype(BF16)
        h_scr[...] = hb
        lf = _log_sigmoid(_dot(hb, wf_ref[...]) + bf_ref[...])
        lf_ref[...] = lf

        @pl.when(i == 0)
        def _():
            carry_scr[...] = jnp.zeros_like(carry_scr)

        rows = lax.broadcasted_iota(jnp.int32, (tm, tm), 0)
        cols = lax.broadcasted_iota(jnp.int32, (tm, tm), 1)
        tri = (cols <= rows).astype(F32)
        csum = _dot(tri, lf, HI) + carry_scr[...]
        fc_ref[...] = csum
        carry_scr[...] = csum[tm - 1:tm, :]

    z = _dot(h_scr[...], w_ref[...])
    z_ref[...] = z
    zb_ref[...] = z.astype(BF16)


def _inproj_prompt(x, sc, sh, g, w6, wf, bfor, tm=512, tn=1024):
    p, d = x.shape
    n = w6.shape[1]
    row = lambda i, j: (i, 0)
    const = lambda i, j: (0, 0)
    return pl.pallas_call(
        _inproj_kernel,
        grid=(p // tm, n // tn),
        in_specs=[pl.BlockSpec((tm, d), row),
                  pl.BlockSpec((1, d), const), pl.BlockSpec((1, d), const), pl.BlockSpec((1, d), const),
                  pl.BlockSpec((d, tn), lambda i, j: (0, j)),
                  pl.BlockSpec((d, LANES), const), pl.BlockSpec((1, LANES), const)],
        out_specs=[pl.BlockSpec((tm, tn), lambda i, j: (i, j)),
                   pl.BlockSpec((tm, tn), lambda i, j: (i, j)),
                   pl.BlockSpec((tm, LANES), row),
                   pl.BlockSpec((tm, LANES), row)],
        out_shape=[jax.ShapeDtypeStruct((p, n), F32), jax.ShapeDtypeStruct((p, n), BF16),
                   jax.ShapeDtypeStruct((p, LANES), F32), jax.ShapeDtypeStruct((p, LANES), F32)],
        scratch_shapes=[pltpu.VMEM((tm, d), BF16), pltpu.VMEM((1, LANES), F32)],
        compiler_params=_cp(("arbitrary", "arbitrary")),
        name="inproj_prompt",
    )(x, sc, sh, g, w6, wf, bfor)


FOX_UNROLL = 4


def _fox_kernel(q_ref, k_ref, v_ref, nf_ref, g_ref, o_ref, m_scr, acc_scr, *, tq):
    i = pl.program_id(1)
    d = HEAD_DIM
    tk = tq
    q = (q_ref[...].astype(F32) * (ATTN_SCALE * LOG2E)).astype(BF16)
    m_scr[...] = jnp.full_like(m_scr, -jnp.inf)
    acc_scr[...] = jnp.zeros_like(acc_scr)
    ones = jnp.ones((tk, d), BF16)

    def update(j, masked):
        k = k_ref[pl.ds(pl.multiple_of(j * tk, tk), tk), :]
        v = v_ref[pl.ds(pl.multiple_of(j * tk, tk), tk), :]
        s = _dot_nt(q, k) + nf_ref[j]
        if masked:
            rows = lax.broadcasted_iota(jnp.int32, (tq, tk), 0)
            cols = lax.broadcasted_iota(jnp.int32, (tq, tk), 1)
            s = jnp.where(cols <= rows, s, -jnp.inf)
        m_prev = m_scr[...]
        m_new = jnp.maximum(m_prev, jnp.max(s, axis=1, keepdims=True))
        alpha = jnp.exp2(m_prev - m_new)
        p = jnp.exp2(s - jnp.tile(m_new, (1, tk // LANES)))
        pv = _dot(p.astype(BF16), jnp.concatenate([v, ones], axis=1))
        acc_scr[...] = jnp.tile(alpha, (1, 2 * d // LANES)) * acc_scr[...] + pv
        m_scr[...] = m_new

    def body_group(jg, c):
        for u in range(FOX_UNROLL):
            update(jg * FOX_UNROLL + u, False)
        return c

    def body_one(j, c):
        update(j, False)
        return c

    n_grouped = (i // FOX_UNROLL) * FOX_UNROLL
    lax.fori_loop(0, i // FOX_UNROLL, body_group, 0)
    lax.fori_loop(n_grouped, i, body_one, 0)
    update(i, True)
    o = acc_scr[:, :d] / acc_scr[:, d:]
    o_ref[...] = _rms(o, g_ref[...]).astype(o_ref.dtype)


def _fox_attn(zb, nf2, g, col_q, col_k, col_v, tq):
    p = zb.shape[0]
    d = HEAD_DIM
    tk = tq
    return pl.pallas_call(
        functools.partial(_fox_kernel, tq=tq),
        grid=(N_FOX, p // tq),
        in_specs=[pl.BlockSpec((tq, d), lambda h, i: (i, col_q + h)),
                  pl.BlockSpec((p, d), lambda h, i: (0, col_k + h)),
                  pl.BlockSpec((p, d), lambda h, i: (0, col_v + h)),
                  pl.BlockSpec((None, p // tk, 1, tk), lambda h, i: (h, 0, 0, 0)),
                  pl.BlockSpec((None, 1, d), lambda h, i: (h, 0, 0))],
        out_specs=pl.BlockSpec((tq, d), lambda h, i: (i, h)),
        out_shape=jax.ShapeDtypeStruct((p, N_FOX * d), BF16),
        scratch_shapes=[pltpu.VMEM((tq, LANES), F32), pltpu.VMEM((tq, 2 * d), F32)],
        compiler_params=_cp(("arbitrary", "arbitrary")),
        name="fox_attn",
    )(zb, zb, zb, nf2, g)


SB_CHUNK = 2 * LANES
SB_UNROLL = 4


def _sb_kernel(q_ref, k_ref, v_ref, b_ref, g_ref, to_ref, o_ref, carry_scr, acc_scr, *, tq):
    i = pl.program_id(1)
    tk = SB_CHUNK
    q = (q_ref[...].astype(F32) * ATTN_SCALE).astype(BF16)
    carry_scr[...] = jnp.zeros_like(carry_scr)
    acc_scr[...] = jnp.zeros_like(acc_scr)

    def step(j, masked):
        k = k_ref[pl.ds(pl.multiple_of(j * tk, tk), tk), :]
        v = v_ref[pl.ds(pl.multiple_of(j * tk, tk), tk), :]
        z = _dot_nt(q, k) + b_ref[...]
        sp = _softplus(z)
        if masked:
            rows = i * tq + lax.broadcasted_iota(jnp.int32, (tq, tk), 0)
            cols = j * tk + lax.broadcasted_iota(jnp.int32, (tq, tk), 1)
            valid = cols < rows
            sp = jnp.where(valid, sp, 0.0)
        carry = carry_scr[...]
        parts = [None, None]
        for half in (1, 0):
            lo_c, hi_c = half * LANES, (half + 1) * LANES
            sph = sp[:, lo_c:hi_c]
            hi = sph.astype(BF16)
            lo = (sph - hi.astype(F32)).astype(BF16)
            r = _dot(jnp.concatenate([hi, lo], axis=1), to_ref[...])
            a = jnp.exp(z[:, lo_c:hi_c] - r[:, :LANES] - carry)
            if masked:
                a = jnp.where(valid[:, lo_c:hi_c], a, 0.0)
            parts[half] = a.astype(BF16)
            carry = carry + r[:, LANES:]
        carry_scr[...] = carry
        acc_scr[...] += _dot(jnp.concatenate(parts, axis=1), v)

    n_full = (i * tq) // tk
    for dgn in reversed(range(tq // tk)):
        step(n_full + dgn, True)

    def body_group(jg, c):
        for u in range(SB_UNROLL):
            step(n_full - 1 - (jg * SB_UNROLL + u), False)
        return c

    def body_one(jj, c):
        step(n_full - 1 - jj, False)
        return c

    n_grouped = (n_full // SB_UNROLL) * SB_UNROLL
    lax.fori_loop(0, n_full // SB_UNROLL, body_group, 0)
    lax.fori_loop(n_grouped, n_full, body_one, 0)
    o_ref[...] = _rms(acc_scr[...], g_ref[...]).astype(o_ref.dtype)


def _sb_tri_ones():
    r = lax.broadcasted_iota(jnp.int32, (LANES, LANES), 0)
    c = lax.broadcasted_iota(jnp.int32, (LANES, LANES), 1)
    blk = jnp.concatenate([(r >= c).astype(BF16), jnp.ones((LANES, LANES), BF16)], axis=1)
    return jnp.concatenate([blk, blk], axis=0)


def _sb_attn(zb, bias, g, col_q, col_k, col_v, tq=512):
    p = zb.shape[0]
    d = HEAD_DIM
    return pl.pallas_call(
        functools.partial(_sb_kernel, tq=tq),
        grid=(N_SB, p // tq),
        in_specs=[pl.BlockSpec((tq, d), lambda h, i: (i, col_q + h)),
                  pl.BlockSpec((p, d), lambda h, i: (0, col_k + h)),
                  pl.BlockSpec((p, d), lambda h, i: (0, col_v + h)),
                  pl.BlockSpec((None, 1, SB_CHUNK), lambda h, i: (h, 0, 0)),
                  pl.BlockSpec((None, 1, d), lambda h, i: (h, 0, 0)),
                  pl.BlockSpec((SB_CHUNK, SB_CHUNK), lambda h, i: (0, 0))],
        out_specs=pl.BlockSpec((tq, d), lambda h, i: (i, h)),
        out_shape=jax.ShapeDtypeStruct((p, N_SB * d), BF16),
        scratch_shapes=[pltpu.VMEM((tq, LANES), F32), pltpu.VMEM((tq, d), F32)],
        compiler_params=_cp(("arbitrary", "arbitrary")),
        name="sb_attn",
    )(zb, zb, zb, bias, g, _sb_tri_ones())


def _mixout_kernel(of_ref, os_ref, x_ref, gt_ref, sc_ref, sh_ref, gpm_ref, gpf_ref, wo_ref, wr_ref, br_ref,
                   x1_ref, h2_ref, idx_ref, gate_ref, rank_ref, wd_ref, cnt_ref, carry_scr):
    i = pl.program_id(0)
    tm = x_ref.shape[0]
    half = of_ref.shape[1]

    @pl.when(i == 0)
    def _():
        carry_scr[...] = jnp.zeros_like(carry_scr)

    mix = (_dot(of_ref[...].astype(BF16), wo_ref[0:half, :])
           + _dot(os_ref[...].astype(BF16), wo_ref[half:2 * half, :]))
    x1 = x_ref[...] + gt_ref[...] * _rms(mix, gpm_ref[...])
    x1_ref[...] = x1
    h2 = _rms(x1, gpf_ref[...]) * (1.0 + sc_ref[...]) + sh_ref[...]
    h2_ref[...] = h2
    logits = _dot(h2.astype(BF16), wr_ref[...]) + br_ref[...]
    lane = lax.broadcasted_iota(jnp.int32, (tm, LANES), 1)
    lane_f = lane.astype(F32)
    vals = logits
    top_i, top_v = [], []
    for _ in range(TOP_K):
        mk = jnp.max(vals, axis=1, keepdims=True)
        ik = jnp.min(jnp.where(vals == mk, lane_f, float(LANES)), axis=1, keepdims=True)
        vals = jnp.where(lane_f == ik, -jnp.inf, vals)
        top_i.append(ik)
        top_v.append(mk)
    ex = [jnp.exp(v - top_v[0]) for v in top_v]
    den = ex[0] + ex[1] + ex[2] + ex[3]
    gates = [e / den for e in ex]
    hot = [lane_f == ik for ik in top_i]
    sel = jnp.zeros((tm, LANES), F32)
    wdense = jnp.zeros((tm, LANES), F32)
    for k in range(TOP_K):
        sel = sel + hot[k].astype(F32)
        wdense = wdense + jnp.where(hot[k], gates[k], 0.0)
    rows = lax.broadcasted_iota(jnp.int32, (tm, tm), 0)
    cols = lax.broadcasted_iota(jnp.int32, (tm, tm), 1)
    before = (cols < rows).astype(BF16)
    cnt = _dot(before, sel.astype(BF16)) + carry_scr[...]
    idx_o = jnp.zeros((tm, LANES), F32)
    gate_o = jnp.zeros((tm, LANES), F32)
    rank_o = jnp.zeros((tm, LANES), F32)
    for k in range(TOP_K):
        rk = jnp.sum(jnp.where(hot[k], cnt, 0.0), axis=1, keepdims=True)
        idx_o = jnp.where(lane == k, top_i[k], idx_o)
        gate_o = jnp.where(lane == k, gates[k], gate_o)
        rank_o = jnp.where(lane == k, rk, rank_o)
    idx_ref[...] = idx_o.astype(jnp.int32)
    gate_ref[...] = gate_o
    rank_ref[...] = rank_o.astype(jnp.int32)
    wd_ref[...] = wdense
    total = carry_scr[...] + jnp.sum(sel, axis=0, keepdims=True)
    carry_scr[...] = total
    cnt_ref[...] = total


def _mixout(of, os_, x, gt, sc, sh, gpm, gpf, wo, wr, br, tm):
    p, d = x.shape
    half = of.shape[1]
    row = lambda i: (i, 0)
    const = lambda i: (0, 0)
    mrow = row if gt.shape[0] == p else const
    msh = (tm, d) if gt.shape[0] == p else (1, d)
    return pl.pallas_call(
        _mixout_kernel,
        grid=(p // tm,),
        in_specs=[pl.BlockSpec((tm, half), row), pl.BlockSpec((tm, half), row), pl.BlockSpec((tm, d), row),
                  pl.BlockSpec(msh, mrow), pl.BlockSpec(msh, mrow), pl.BlockSpec(msh, mrow),
                  pl.BlockSpec((1, d), const), pl.BlockSpec((1, d), const),
                  pl.BlockSpec((2 * half, d), const),
                  pl.BlockSpec((d, LANES), const), pl.BlockSpec((1, LANES), const)],
        out_specs=[pl.BlockSpec((tm, d), row), pl.BlockSpec((tm, d), row),
                   pl.BlockSpec((tm, LANES), row), pl.BlockSpec((tm, LANES), row),
                   pl.BlockSpec((tm, LANES), row), pl.BlockSpec((tm, LANES), row),
                   pl.BlockSpec((1, LANES), const)],
        out_shape=[jax.ShapeDtypeStruct((p, d), F32), jax.ShapeDtypeStruct((p, d), F32),
                   jax.ShapeDtypeStruct((p, LANES), jnp.int32), jax.ShapeDtypeStruct((p, LANES), F32),
                   jax.ShapeDtypeStruct((p, LANES), jnp.int32), jax.ShapeDtypeStruct((p, LANES), F32),
                   jax.ShapeDtypeStruct((1, LANES), F32)],
        scratch_shapes=[pltpu.VMEM((1, LANES), F32)],
        compiler_params=_cp(("arbitrary",)),
        name="mixout_router",
    )(of, os_, x, gt, sc, sh, gpm, gpf, wo, wr, br)


def _dispatch_kernel(dest_ref, h_ref, xs_in, xs_out, sem, *, tt):
    del xs_in

    def copy(t, k):
        d = dest_ref[t * TOP_K + k]
        return pltpu.make_async_copy(h_ref.at[pl.ds(t, 1)], xs_out.at[pl.ds(d, 1)], sem)

    def start(t, c):
        for k in range(TOP_K):
            copy(t, k).start()
        return c

    def wait(t, c):
        for k in range(TOP_K):
            copy(t, k).wait()
        return c

    lax.fori_loop(0, tt, start, 0)
    lax.fori_loop(0, tt, wait, 0)


def _dispatch(dest_flat, h2, xs_zero, tt=128):
    p, d = h2.shape
    return pl.pallas_call(
        functools.partial(_dispatch_kernel, tt=tt),
        grid=(p // tt,),
        in_specs=[pl.BlockSpec((tt * TOP_K,), lambda i: (i,), memory_space=pltpu.SMEM),
                  pl.BlockSpec((tt, d), lambda i: (i, 0)),
                  pl.BlockSpec(memory_space=pl.ANY)],
        out_specs=pl.BlockSpec(memory_space=pl.ANY),
        out_shape=jax.ShapeDtypeStruct(xs_zero.shape, xs_zero.dtype),
        scratch_shapes=[pltpu.SemaphoreType.DMA(())],
        input_output_aliases={2: 0},
        compiler_params=_cp(("arbitrary",)),
        name="moe_dispatch",
    )(dest_flat, h2, xs_zero)


def _gate_up_kernel(te_ref, nu_ref, x_ref, wg_ref, wu_ref, bg_ref, bu_ref, h_ref, wgb, wub):
    i = pl.program_id(1)

    @pl.when(i < nu_ref[0])
    def _():
        prev = te_ref[jnp.maximum(i - 1, 0)]
        fresh = jnp.logical_or(i == 0, te_ref[i] != prev)

        @pl.when(fresh)
        def _():
            wgb[...] = wg_ref[...].astype(BF16)
            wub[...] = wu_ref[...].astype(BF16)

        x = x_ref[...].astype(BF16)
        g = jnp.minimum(_dot(x, wgb[...]) + bg_ref[...], SWIGLU_LIMIT)
        u = jnp.clip(_dot(x, wub[...]) + bu_ref[...], -SWIGLU_LIMIT, SWIGLU_LIMIT)
        act = g * (1.0 / (1.0 + jnp.exp(-SWIGLU_ALPHA * g))) * (u + 1.0)
        h_ref[...] = act.astype(h_ref.dtype)

    @pl.when(i >= nu_ref[0])
    def _():
        h_ref[...] = jnp.zeros_like(h_ref)


def _gate_up(tile_e, n_used, xs, w_gu, b_gu, tm, tn=512):
    r, d = xs.shape
    f = w_gu.shape[2] // 2
    nb = f // tn

    def tile(n, i, te, nu):
        return jnp.minimum(i, nu[0] - 1)

    return pl.pallas_call(
        _gate_up_kernel,
        grid_spec=pltpu.PrefetchScalarGridSpec(
            num_scalar_prefetch=2,
            grid=(nb, r // tm),
            in_specs=[pl.BlockSpec((tm, d), lambda n, i, te, nu: (tile(n, i, te, nu), 0)),
                      pl.BlockSpec((None, d, tn), lambda n, i, te, nu: (te[tile(n, i, te, nu)], 0, n)),
                      pl.BlockSpec((None, d, tn), lambda n, i, te, nu: (te[tile(n, i, te, nu)], 0, nb + n)),
                      pl.BlockSpec((None, 1, tn), lambda n, i, te, nu: (te[tile(n, i, te, nu)], 0, n)),
                      pl.BlockSpec((None, 1, tn), lambda n, i, te, nu: (te[tile(n, i, te, nu)], 0, nb + n))],
            out_specs=pl.BlockSpec((tm, tn), lambda n, i, te, nu: (i, n)),
            scratch_shapes=[pltpu.VMEM((d, tn), BF16), pltpu.VMEM((d, tn), BF16)]),
        out_shape=jax.ShapeDtypeStruct((r, f), BF16),
        compiler_params=_cp(("arbitrary", "arbitrary")),
        name="moe_gate_up",
    )(tile_e, n_used, xs, w_gu, w_gu, b_gu, b_gu)


def _down_kernel(te_ref, nu_ref, h_ref, w_ref, b_ref, y_ref, wb):
    i = pl.program_id(1)

    @pl.when(i < nu_ref[0])
    def _():
        prev = te_ref[jnp.maximum(i - 1, 0)]
        fresh = jnp.logical_or(i == 0, te_ref[i] != prev)

        @pl.when(fresh)
        def _():
            wb[...] = w_ref[...].astype(BF16)

        y_ref[...] = _dot(h_ref[...], wb[...]) + b_ref[...]

    @pl.when(i >= nu_ref[0])
    def _():
        y_ref[...] = jnp.zeros_like(y_ref)


def _down(tile_e, n_used, hs, w_dn, b_dn, tm, tn=1024):
    r, f = hs.shape
    d = w_dn.shape[2]

    def tile(n, i, te, nu):
        return jnp.minimum(i, nu[0] - 1)

    return pl.pallas_call(
        _down_kernel,
        grid_spec=pltpu.PrefetchScalarGridSpec(
            num_scalar_prefetch=2,
            grid=(d // tn, r // tm),
            in_specs=[pl.BlockSpec((tm, f), lambda n, i, te, nu: (tile(n, i, te, nu), 0)),
                      pl.BlockSpec((None, f, tn), lambda n, i, te, nu: (te[tile(n, i, te, nu)], 0, n)),
                      pl.BlockSpec((None, 1, tn), lambda n, i, te, nu: (te[tile(n, i, te, nu)], 0, n))],
            out_specs=pl.BlockSpec((tm, tn), lambda n, i, te, nu: (i, n)),
            scratch_shapes=[pltpu.VMEM((f, tn), BF16)]),
        out_shape=jax.ShapeDtypeStruct((r, d), F32),
        compiler_params=_cp(("arbitrary", "arbitrary")),
        name="moe_down",
    )(tile_e, n_used, hs, w_dn, b_dn)


def _combine_kernel(dest_ref, y_hbm, gate_ref, x1_ref, gt_ref, g_ref, o_ref, buf, sem, *, tt):
    def copy(t, k):
        d = dest_ref[t * TOP_K + k]
        return pltpu.make_async_copy(y_hbm.at[pl.ds(d, 1)], buf.at[k, pl.ds(t, 1)], sem)

    def start(t, c):
        for k in range(TOP_K):
            copy(t, k).start()
        return c

    def wait(t, c):
        for k in range(TOP_K):
            copy(t, k).wait()
        return c

    lax.fori_loop(0, tt, start, 0)
    lax.fori_loop(0, tt, wait, 0)
    gate = gate_ref[...]
    ffn = gate[:, 0:1] * buf[0]
    for k in range(1, TOP_K):
        ffn = ffn + gate[:, k:k + 1] * buf[k]
    o_ref[...] = x1_ref[...] + gt_ref[...] * _rms(ffn, g_ref[...])


def _combine(dest_flat, ys, gates, x1, gt, g, tt=128):
    p, d = x1.shape
    row = lambda i: (i, 0)
    const = lambda i: (0, 0)
    return pl.pallas_call(
        functools.partial(_combine_kernel, tt=tt),
        grid=(p // tt,),
        in_specs=[pl.BlockSpec((tt * TOP_K,), lambda i: (i,), memory_space=pltpu.SMEM),
                  pl.BlockSpec(memory_space=pl.ANY),
                  pl.BlockSpec((tt, LANES), row), pl.BlockSpec((tt, d), row),
                  pl.BlockSpec((1, d), const), pl.BlockSpec((1, d), const)],
        out_specs=pl.BlockSpec((tt, d), row),
        out_shape=jax.ShapeDtypeStruct((p, d), F32),
        scratch_shapes=[pltpu.VMEM((TOP_K, tt, d), F32), pltpu.SemaphoreType.DMA(())],
        compiler_params=_cp(("arbitrary",)),
        name="moe_combine",
    )(dest_flat, ys, gates, x1, gt, g)


def _moe_prompt(h2, top_idx, gates, rank, counts, x1, gt_m, g_post, w_gu, b_gu, w_dn, b_dn, tm=512):
    p, d = h2.shape
    n_tiles = (p * TOP_K) // tm + N_EXPERTS
    padded = (counts + tm - 1) // tm * tm
    pends = jnp.cumsum(padded)
    pstart = pends - padded
    dest = (pstart[top_idx] + rank).reshape(-1).astype(jnp.int32)
    bstart = jnp.arange(n_tiles, dtype=jnp.int32) * tm
    tile_e = jnp.minimum(jnp.sum(pends[None, :] <= bstart[:, None], axis=1), N_EXPERTS - 1).astype(jnp.int32)
    n_used = (pends[-1:] // tm).astype(jnp.int32)
    xs = _dispatch(dest, h2, jnp.zeros((n_tiles * tm, d), F32))
    hs = _gate_up(tile_e, n_used, xs, w_gu, b_gu[:, None, :], tm)
    ys = _down(tile_e, n_used, hs, w_dn, b_dn[:, None, :], tm)
    return _combine(dest, ys, gates, x1, gt_m, g_post)


def _inproj_sample_kernel(x_ref, sc_ref, sh_ref, g_ref, w_ref, bf_ref, z_ref, lf_ref, h_scr, *, nk, tk, off_f):
    k = pl.program_id(0)

    @pl.when(k == 0)
    def _():
        h = _rms(x_ref[...], g_ref[...]) * (1.0 + sc_ref[...]) + sh_ref[...]
        for kk in range(nk):
            h_scr[kk] = h[:, kk * tk:(kk + 1) * tk]
        z_ref[...] = jnp.zeros_like(z_ref)

    z_ref[...] += _dot(h_scr[k].astype(BF16), w_ref[...].astype(BF16))

    @pl.when(k == nk - 1)
    def _():
        lf_ref[...] = _log_sigmoid(z_ref[:, off_f:off_f + LANES] + bf_ref[...])


def _inproj_sample(x, sc, sh, g, w, bfor, off_f, tk=256):
    n, d = x.shape
    dn = w.shape[1]
    nk = d // tk
    const = lambda k: (0, 0)
    return pl.pallas_call(
        functools.partial(_inproj_sample_kernel, nk=nk, tk=tk, off_f=off_f),
        grid=(nk,),
        in_specs=[pl.BlockSpec((n, d), const), pl.BlockSpec((n, d), const), pl.BlockSpec((n, d), const),
                  pl.BlockSpec((1, d), const), pl.BlockSpec((tk, dn), lambda k: (k, 0)),
                  pl.BlockSpec((1, LANES), const)],
        out_specs=[pl.BlockSpec((n, dn), const), pl.BlockSpec((n, LANES), const)],
        out_shape=[jax.ShapeDtypeStruct((n, dn), F32), jax.ShapeDtypeStruct((n, LANES), F32)],
        scratch_shapes=[pltpu.VMEM((nk, n, tk), F32)],
        compiler_params=_cp(("arbitrary",)),
        name="inproj_sample",
    )(x, sc, sh, g, w, bfor)


def _decode_kernel(pt_ref, fq_ref, fkn_ref, fvn_ref, lfn_ref, sq_ref, bsb_ref, gf_ref, gs_ref,
                   fk_hbm, fv_hbm, lf_hbm, sk_hbm, sv_hbm, of_ref, os_ref,
                   buf, lfbuf, sems, lsem, m_scr, l_scr, acc_scr, dsuf_scr, sacc_scr, skeep_scr, *, n_pages):
    n = pl.program_id(0)
    grp = buf.shape[1]
    page = buf.shape[3]
    nh = fq_ref.shape[0]
    d = HEAD_DIM
    n_groups = n_pages // grp
    caches = (fk_hbm, fv_hbm, sk_hbm, sv_hbm)

    def fetch(gi, slot):
        cps = []
        for g in range(grp):
            pg = pt_ref[n, n_pages - 1 - gi * grp - g]
            cps += [pltpu.make_async_copy(hbm.at[pg, :, h, :], buf.at[slot, g, c, :, pl.ds(h * d, d)],
                                          sems.at[slot, c])
                    for c, hbm in enumerate(caches) for h in range(nh)]
            cps.append(pltpu.make_async_copy(lf_hbm.at[pg], lfbuf.at[slot, g], lsem.at[slot]))
        return cps

    for cp in fetch(0, 0):
        cp.start()

    sub = lax.broadcasted_iota(jnp.int32, (nh, d), 0)
    sub_w = lax.broadcasted_iota(jnp.int32, (nh, nh * d), 0)
    blk_w = lax.broadcasted_iota(jnp.int32, (nh, nh * d), 1) // d

    def block_diag(q):
        return jnp.where(sub_w == blk_w, jnp.concatenate([q] * nh, axis=1), jnp.zeros((), q.dtype))

    def diag_blocks(o_full):
        o = jnp.zeros((nh, d), F32)
        for h in range(nh):
            o = jnp.where(sub == h, o_full[:, h * d:(h + 1) * d], o)
        return o

    q = fq_ref[...].astype(BF16)
    qf_bd = block_diag(q)
    qs_bd = block_diag(sq_ref[...].astype(BF16))
    kn = fkn_ref[...].astype(BF16).astype(F32)
    m_scr[...] = jnp.sum(q.astype(F32) * kn, axis=1, keepdims=True) * ATTN_SCALE
    l_scr[...] = jnp.ones_like(l_scr)
    acc_scr[...] = fvn_ref[...].astype(BF16).astype(F32)
    dsuf_scr[...] = jnp.zeros_like(dsuf_scr)
    sacc_scr[...] = jnp.zeros_like(sacc_scr)
    skeep_scr[...] = jnp.zeros_like(skeep_scr)

    rr = lax.broadcasted_iota(jnp.int32, (2 * page, page), 0) % page
    cc = lax.broadcasted_iota(jnp.int32, (2 * page, page), 1)
    later2 = (rr > cc).astype(BF16)

    def later_sums(x):
        hi = x.astype(BF16)
        lo = (x - hi.astype(F32)).astype(BF16)
        return _dot(jnp.concatenate([hi, lo], axis=1), later2)

    def body(gi, carry):
        slot = gi % 2

        @pl.when(gi + 1 < n_groups)
        def _():
            for cp in fetch(gi + 1, 1 - slot):
                cp.start()

        for cp in fetch(gi, slot):
            cp.wait()

        dsuf = dsuf_scr[...]
        scs, pvs = [], None
        for g in range(grp):
            lf = lfbuf[slot, g]
            s = _dot_nt(qf_bd, buf[slot, g, 0].astype(BF16)) * ATTN_SCALE
            scs.append(s + lfn_ref[...] + later_sums(lf) + dsuf)
            dsuf = dsuf + jnp.sum(lf, axis=1, keepdims=True)
        dsuf_scr[...] = dsuf
        sc = jnp.concatenate(scs, axis=1)
        m_prev = m_scr[...]
        m_new = jnp.maximum(m_prev, jnp.max(sc, axis=1, keepdims=True))
        corr = jnp.exp(m_prev - m_new)
        pr = jnp.exp(sc - m_new)
        l_scr[...] = l_scr[...] * corr + jnp.sum(pr, axis=1, keepdims=True)
        for g in range(grp):
            o = _dot(pr[:, g * page:(g + 1) * page].astype(BF16), buf[slot, g, 1].astype(BF16))
            pvs = o if pvs is None else pvs + o
        acc_scr[...] = acc_scr[...] * corr + diag_blocks(pvs)
        m_scr[...] = m_new

        keep = skeep_scr[...]
        avs = None
        for g in range(grp):
            z = _dot_nt(qs_bd, buf[slot, g, 2].astype(BF16)) * ATTN_SCALE + bsb_ref[...]
            sp = _softplus(z)
            a = jnp.exp(z - sp - later_sums(sp) - keep)
            keep = keep + jnp.sum(sp, axis=1, keepdims=True)
            o = _dot(a.astype(BF16), buf[slot, g, 3].astype(BF16))
            avs = o if avs is None else avs + o
        skeep_scr[...] = keep
        sacc_scr[...] += diag_blocks(avs)
        return carry

    lax.fori_loop(0, n_groups, body, 0)
    of_ref[...] = _rms(acc_scr[...] / l_scr[...], gf_ref[...])
    os_ref[...] = _rms(sacc_scr[...], gs_ref[...])


def _decode(page_table, fq, fkn, fvn, lfn, sq, bsb, gf, gs, ck, cv, clf_t, csk, csv):
    n, n_pages = page_table.shape
    grp = math.gcd(n_pages, 4)
    nh, d = fq.shape[1], fq.shape[2]
    page = ck.shape[1]
    per_seq = lambda i, pt: (i, 0, 0)
    const = lambda i, pt: (0, 0)
    vec = pl.BlockSpec((None, nh, d), per_seq)
    col = pl.BlockSpec((None, nh, 1), per_seq)
    hbm = pl.BlockSpec(memory_space=pl.ANY)
    return pl.pallas_call(
        functools.partial(_decode_kernel, n_pages=n_pages),
        grid_spec=pltpu.PrefetchScalarGridSpec(
            num_scalar_prefetch=1,
            grid=(n,),
            in_specs=[vec, vec, vec, col, vec,
                      pl.BlockSpec((nh, 1), const), pl.BlockSpec((nh, d), const), pl.BlockSpec((nh, d), const),
                      hbm, hbm, hbm, hbm, hbm],
            out_specs=[vec, vec],
            scratch_shapes=[pltpu.VMEM((2, grp, 4, page, nh * d), F32), pltpu.VMEM((2, grp, nh, page), F32),
                            pltpu.SemaphoreType.DMA((2, 4)), pltpu.SemaphoreType.DMA((2,)),
                            pltpu.VMEM((nh, 1), F32), pltpu.VMEM((nh, 1), F32), pltpu.VMEM((nh, d), F32),
                            pltpu.VMEM((nh, 1), F32), pltpu.VMEM((nh, d), F32), pltpu.VMEM((nh, 1), F32)]),
        out_shape=[jax.ShapeDtypeStruct((n, nh, d), F32), jax.ShapeDtypeStruct((n, nh, d), F32)],
        compiler_params=_cp(("arbitrary",)),
        name="decode_attn",
    )(page_table, fq, fkn, fvn, lfn, sq, bsb, gf, gs, ck, cv, clf_t, csk, csv)


def _moe_small_kernel(h_ref, wd_ref, wg_ref, wu_ref, bg_ref, bu_ref, wdn_ref, bd_ref, x1_ref, gt_ref, g_ref,
                      y_ref, acc_scr, *, n_e, n_j):
    e = pl.program_id(0)
    j = pl.program_id(1)
    n = h_ref.shape[0]

    @pl.when(jnp.logical_and(e == 0, j == 0))
    def _():
        acc_scr[...] = jnp.zeros_like(acc_scr)

    lane = lax.broadcasted_iota(jnp.int32, (n, LANES), 1)
    we = jnp.sum(jnp.where(lane == e, wd_ref[...], 0.0), axis=1, keepdims=True)
    hb = h_ref[...].astype(BF16)
    g = jnp.minimum(_dot(hb, wg_ref[...].astype(BF16)) + bg_ref[...], SWIGLU_LIMIT)
    u = jnp.clip(_dot(hb, wu_ref[...].astype(BF16)) + bu_ref[...], -SWIGLU_LIMIT, SWIGLU_LIMIT)
    act = g * (1.0 / (1.0 + jnp.exp(-SWIGLU_ALPHA * g))) * (u + 1.0)
    y = _dot(act.astype(BF16), wdn_ref[...].astype(BF16))
    acc_scr[...] += we * y

    @pl.when(j == 0)
    def _():
        acc_scr[...] += we * bd_ref[...]

    @pl.when(jnp.logical_and(e == n_e - 1, j == n_j - 1))
    def _():
        y_ref[...] = x1_ref[...] + gt_ref[...] * _rms(acc_scr[...], g_ref[...])


def _moe_small(h2, wdense, w_gu, b_gu, w_dn, b_dn, x1, gt, g, tf=512):
    n, d = h2.shape
    n_e = w_gu.shape[0]
    f = w_gu.shape[2] // 2
    nj = f // tf
    const = lambda e, j: (0, 0)
    return pl.pallas_call(
        functools.partial(_moe_small_kernel, n_e=n_e, n_j=nj),
        grid=(n_e, nj),
        in_specs=[pl.BlockSpec((n, d), const), pl.BlockSpec((n, LANES), const),
                  pl.BlockSpec((None, d, tf), lambda e, j: (e, 0, j)),
                  pl.BlockSpec((None, d, tf), lambda e, j: (e, 0, nj + j)),
                  pl.BlockSpec((None, 1, tf), lambda e, j: (e, 0, j)),
                  pl.BlockSpec((None, 1, tf), lambda e, j: (e, 0, nj + j)),
                  pl.BlockSpec((None, tf, d), lambda e, j: (e, j, 0)),
                  pl.BlockSpec((None, 1, d), lambda e, j: (e, 0, 0)),
                  pl.BlockSpec((n, d), const), pl.BlockSpec((n, d), const), pl.BlockSpec((1, d), const)],
        out_specs=pl.BlockSpec((n, d), const),
        out_shape=jax.ShapeDtypeStruct((n, d), F32),
        scratch_shapes=[pltpu.VMEM((n, d), F32)],
        compiler_params=_cp(("arbitrary", "arbitrary")),
        name="moe_sample",
    )(h2, wdense, w_gu, w_gu, b_gu, b_gu, w_dn, b_dn, x1, gt, g)


def _pad_lanes(a, fill=0.0):
    return jnp.pad(a, ((0, 0), (0, LANES - a.shape[1])), constant_values=fill)


def kernel(x_prompt, x_sample, cache_fox_k, cache_fox_v, cache_fox_logf, cache_sb_k, cache_sb_v, page_table,
           c_prompt, c_sample, w_ada, b_ada, g_pre_mix, w_in, b_forget, b_sb, g_fox_out, g_sb_out, w_out,
           g_post_mix, g_pre_ffn, w_router, b_router, w_gate_up, b_gate_up, w_down, b_down, g_post_ffn):
    depth = w_ada.shape[0]
    assert depth == 1 and x_sample.shape[1] == 1 and x_prompt.shape[0] == 1
    bsz, seq, d = x_prompt.shape
    ns = x_sample.shape[0]
    p = bsz * seq
    dfox = N_FOX * HEAD_DIM
    dsb = N_SB * HEAD_DIM
    off_ff = 3 * dfox
    off_sq = off_ff + N_FOX
    xp = x_prompt.reshape(p, d)
    xs = x_sample.reshape(ns, d)

    n_c = bsz + ns
    c_all = jnp.concatenate([c_prompt, c_sample, jnp.zeros((-n_c % 8, d), F32)], axis=0)
    mod = _ada(c_all, w_ada[0], b_ada[0][None])
    sh_a, sc_a, gt_a, sh_m, sc_m, gt_m = [mod[0:bsz, k * d:(k + 1) * d] for k in range(6)]
    ssh_a, ssc_a, sgt_a, ssh_m, ssc_m, sgt_m = [mod[bsz:n_c, k * d:(k + 1) * d] for k in range(6)]

    w_in0 = w_in[0]
    bfor = _pad_lanes(b_forget[0][None])
    gpre = g_pre_mix[0][None]
    gf = g_fox_out[0]
    gs = g_sb_out[0]
    wr = _pad_lanes(w_router[0]).astype(BF16)
    wo = w_out[0].astype(BF16)
    br = _pad_lanes(b_router[0][None], NEG_BIG)
    gpm = g_post_mix[0][None]
    gpf = g_pre_ffn[0][None]
    gpo = g_post_ffn[0][None]

    w6 = jnp.concatenate([w_in0[:, :off_ff], w_in0[:, off_sq:]], axis=1).astype(BF16)
    wf = _pad_lanes(w_in0[:, off_ff:off_sq]).astype(BF16)
    z, zb, lf, fcum = _inproj_prompt(xp, sc_a, sh_a, gpre, w6, wf, bfor)
    tkf = 512
    nf2 = (-LOG2E * fcum[:, :N_FOX]).T.reshape(N_FOX, p // tkf, 1, tkf)
    hb = dfox // HEAD_DIM
    o_fox = _fox_attn(zb, nf2, gf[:, None, :], 0, hb, 2 * hb, tq=tkf)
    bias_sb = jnp.broadcast_to(b_sb[0][:, None, None], (N_SB, 1, SB_CHUNK))
    o_sb = _sb_attn(zb, bias_sb, gs[:, None, :], 3 * hb, 4 * hb, 5 * hb)
    x1, h2, top_idx, gates, rank, _, counts = _mixout(
        o_fox, o_sb, xp, gt_a, sc_m, sh_m, gpm, gpf, wo, wr, br, tm=256)
    cnt = counts[0, :N_EXPERTS].astype(jnp.int32)
    y_p = _moe_prompt(h2, top_idx[:, :TOP_K], gates, rank[:, :TOP_K], cnt, x1, gt_m, gpo,
                      w_gate_up[0], b_gate_up[0], w_down[0], b_down[0])

    zs, lfs = _inproj_sample(xs, ssc_a, ssh_a, gpre, w_in0, bfor, off_ff)
    heads = lambda a: a.reshape(ns, -1, HEAD_DIM)
    fq_s, fk_s, fv_s = heads(zs[:, :dfox]), heads(zs[:, dfox:2 * dfox]), heads(zs[:, 2 * dfox:off_ff])
    sq_s = heads(zs[:, off_sq:off_sq + dsb])
    sk_s = heads(zs[:, off_sq + dsb:off_sq + 2 * dsb])
    sv_s = heads(zs[:, off_sq + 2 * dsb:])
    lf_s = lfs[:, :N_FOX]
    drop0 = lambda a: a.reshape(a.shape[1:])
    of_s, os_s = _decode(page_table, fq_s, fk_s, fv_s, lf_s[:, :, None], sq_s, b_sb[0][:, None], gf, gs,
                         drop0(cache_fox_k), drop0(cache_fox_v), jnp.swapaxes(drop0(cache_fox_logf), 1, 2),
                         drop0(cache_sb_k), drop0(cache_sb_v))
    x1s, h2s, _, _, _, wdense, _ = _mixout(
        of_s.reshape(ns, dfox), os_s.reshape(ns, dsb), xs, sgt_a, ssc_m, ssh_m, gpm, gpf, wo, wr, br, tm=ns)
    y_s = _moe_small(h2s, wdense, w_gate_up[0], b_gate_up[0][:, None, :], w_down[0], b_down[0][:, None, :],
                     x1s, sgt_m, gpo)

    rows5 = lambda a, nh: a.reshape(1, bsz, seq, nh, HEAD_DIM)
    srows5 = lambda a: a.reshape(1, ns, 1, -1, HEAD_DIM)
    return (y_p.reshape(bsz, seq, d), y_s.reshape(ns, 1, d),
            rows5(z[:, dfox:2 * dfox], N_FOX), rows5(z[:, 2 * dfox:3 * dfox], N_FOX),
            lf[:, :N_FOX].reshape(1, bsz, seq, N_FOX),
            rows5(z[:, 3 * dfox + dsb:3 * dfox + 2 * dsb], N_SB), rows5(z[:, 3 * dfox + 2 * dsb:], N_SB),
            srows5(fk_s), srows5(fv_s), lf_s.reshape(1, ns, 1, N_FOX), srows5(sk_s), srows5(sv_s))
```

```python
import functools
import math

import jax
import jax.numpy as jnp
from jax import lax
from jax.experimental import pallas as pl
from jax.experimental.pallas import tpu as pltpu

F32 = jnp.float32
BF16 = jnp.bfloat16
HI = lax.Precision.HIGHEST

HEAD_DIM = 128
N_FOX = 8
N_SB = 8
TOP_K = 4
N_EXPERTS = 32
LANES = 128
NORM_EPS = 1e-6
SWIGLU_ALPHA = 1.702
SWIGLU_LIMIT = 7.0
ATTN_SCALE = HEAD_DIM ** -0.5
LOG2E = math.log2(math.e)
VMEM_LIMIT = 56 * 1024 * 1024
NEG_BIG = -1e30


def _cp(sem, vmem=VMEM_LIMIT, flags=None):
    return pltpu.CompilerParams(dimension_semantics=sem, vmem_limit_bytes=vmem, flags=flags)


def _rms(x, g):
    return x * lax.rsqrt(jnp.mean(x * x, axis=-1, keepdims=True) + NORM_EPS) * g


def _log_sigmoid(x):
    return jnp.minimum(x, 0.0) - jnp.log(1.0 + jnp.exp(-jnp.abs(x)))


def _softplus(x):
    return jnp.maximum(x, 0.0) + jnp.log(1.0 + jnp.exp2(jnp.abs(x) * (-LOG2E)))


def _dot(a, b, precision=None):
    return jnp.dot(a, b, preferred_element_type=F32, precision=precision)


def _dot_nt(a, b):
    return lax.dot_general(a, b, (((1,), (1,)), ((), ())), preferred_element_type=F32)


def _ada_kernel(c_ref, w_ref, b_ref, o_ref):
    c = c_ref[...]
    s = c / (1.0 + jnp.exp(-c))
    o_ref[...] = _dot(s.astype(BF16), w_ref[...].astype(BF16)) + b_ref[...]


def _ada(c_all, w, b, tn=1024):
    r, d = c_all.shape
    n = w.shape[1]
    return pl.pallas_call(
        _ada_kernel,
        grid=(n // tn,),
        in_specs=[pl.BlockSpec((r, d), lambda j: (0, 0)),
                  pl.BlockSpec((d, tn), lambda j: (0, j)),
                  pl.BlockSpec((1, tn), lambda j: (0, j))],
        out_specs=pl.BlockSpec((r, tn), lambda j: (0, j)),
        out_shape=jax.ShapeDtypeStruct((r, n), F32),
        compiler_params=_cp(("arbitrary",)),
        name="ada_mod",
    )(c_all, w, b)


def _inproj_kernel(x_ref, sc_ref, sh_ref, g_ref, w_ref, wf_ref, bf_ref,
                   z_ref, zb_ref, lf_ref, fc_ref, h_scr, carry_scr):
    i = pl.program_id(0)
    j = pl.program_id(1)
    tm = x_ref.shape[0]

    @pl.when(j == 0)
    def _():
        h = _rms(x_ref[...], g_ref[...]) * (1.0 + sc_ref[...]) + sh_ref[...]
        hb = h.astype(BF16)
        h_scr[...] = hb
        lf = _log_sigmoid(_dot(hb, wf_ref[...]) + bf_ref[...])
        lf_ref[...] = lf

        @pl.when(i == 0)
        def _():
            carry_scr[...] = jnp.zeros_like(carry_scr)

        rows = lax.broadcasted_iota(jnp.int32, (tm, tm), 0)
        cols = lax.broadcasted_iota(jnp.int32, (tm, tm), 1)
        tri = (cols <= rows).astype(F32)
        csum = _dot(tri, lf, HI) + carry_scr[...]
        fc_ref[...] = csum
        carry_scr[...] = csum[tm - 1:tm, :]

    z = _dot(h_scr[...], w_ref[...])
    z_ref[...] = z
    zb_ref[...] = z.astype(BF16)


def _inproj_prompt(x, sc, sh, g, w6, wf, bfor, tm=512, tn=1024):
    p, d = x.shape
    n = w6.shape[1]
    row = lambda i, j: (i, 0)
    const = lambda i, j: (0, 0)
    return pl.pallas_call(
        _inproj_kernel,
        grid=(p // tm, n // tn),
        in_specs=[pl.BlockSpec((tm, d), row),
                  pl.BlockSpec((1, d), const), pl.BlockSpec((1, d), const), pl.BlockSpec((1, d), const),
                  pl.BlockSpec((d, tn), lambda i, j: (0, j)),
                  pl.BlockSpec((d, LANES), const), pl.BlockSpec((1, LANES), const)],
        out_specs=[pl.BlockSpec((tm, tn), lambda i, j: (i, j)),
                   pl.BlockSpec((tm, tn), lambda i, j: (i, j)),
                   pl.BlockSpec((tm, LANES), row),
                   pl.BlockSpec((tm, LANES), row)],
        out_shape=[jax.ShapeDtypeStruct((p, n), F32), jax.ShapeDtypeStruct((p, n), BF16),
                   jax.ShapeDtypeStruct((p, LANES), F32), jax.ShapeDtypeStruct((p, LANES), F32)],
        scratch_shapes=[pltpu.VMEM((tm, d), BF16), pltpu.VMEM((1, LANES), F32)],
        compiler_params=_cp(("arbitrary", "arbitrary")),
        name="inproj_prompt",
    )(x, sc, sh, g, w6, wf, bfor)


CHUNK_UNROLLS = (8, 2, 1)


def _unrolled_loops(n, step):
    done = 0
    for u in CHUNK_UNROLLS:
        trips = (n - done) // u

        def body(g, c, u=u, done=done):
            for k in range(u):
                step(done + g * u + k)
            return c

        lax.fori_loop(0, trips, body, 0)
        done = done + trips * u


def _fox_kernel(q_ref, k_ref, v_ref, nf_ref, g_ref, o_ref, m_scr, acc_scr, *, tq):
    i = pl.program_id(1)
    d = HEAD_DIM
    tk = tq
    q = (q_ref[...].astype(F32) * (ATTN_SCALE * LOG2E)).astype(BF16)
    m_scr[...] = jnp.full_like(m_scr, -jnp.inf)
    acc_scr[...] = jnp.zeros_like(acc_scr)
    ones = jnp.ones((tk, d), BF16)

    def update(j, masked):
        k = k_ref[pl.ds(pl.multiple_of(j * tk, tk), tk), :]
        v = v_ref[pl.ds(pl.multiple_of(j * tk, tk), tk), :]
        s = _dot_nt(q, k) + nf_ref[j]
        if masked:
            rows = lax.broadcasted_iota(jnp.int32, (tq, tk), 0)
            cols = lax.broadcasted_iota(jnp.int32, (tq, tk), 1)
            s = jnp.where(cols <= rows, s, -jnp.inf)
        m_prev = m_scr[...]
        m_new = jnp.maximum(m_prev, jnp.max(s, axis=1, keepdims=True))
        alpha = jnp.exp2(m_prev - m_new)
        p = jnp.exp2(s - jnp.tile(m_new, (1, tk // LANES)))
        pv = _dot(p.astype(BF16), jnp.concatenate([v, ones], axis=1))
        acc_scr[...] = jnp.tile(alpha, (1, 2 * d // LANES)) * acc_scr[...] + pv
        m_scr[...] = m_new

    _unrolled_loops(i, lambda j: update(j, False))
    update(i, True)
    o = acc_scr[:, :d] / acc_scr[:, d:]
    o_ref[...] = _rms(o, g_ref[...]).astype(o_ref.dtype)


def _fox_attn(zb, nf2, g, col_q, col_k, col_v, tq):
    p = zb.shape[0]
    d = HEAD_DIM
    tk = tq
    return pl.pallas_call(
        functools.partial(_fox_kernel, tq=tq),
        grid=(N_FOX, p // tq),
        in_specs=[pl.BlockSpec((tq, d), lambda h, i: (i, col_q + h)),
                  pl.BlockSpec((p, d), lambda h, i: (0, col_k + h)),
                  pl.BlockSpec((p, d), lambda h, i: (0, col_v + h)),
                  pl.BlockSpec((None, p // tk, 1, tk), lambda h, i: (h, 0, 0, 0)),
                  pl.BlockSpec((None, 1, d), lambda h, i: (h, 0, 0))],
        out_specs=pl.BlockSpec((tq, d), lambda h, i: (i, h)),
        out_shape=jax.ShapeDtypeStruct((p, N_FOX * d), BF16),
        scratch_shapes=[pltpu.VMEM((tq, LANES), F32), pltpu.VMEM((tq, 2 * d), F32)],
        compiler_params=_cp(("arbitrary", "arbitrary")),
        name="fox_attn",
    )(zb, zb, zb, nf2, g)


SB_CHUNK = 2 * LANES


def _sb_kernel(q_ref, k_ref, v_ref, b_ref, g_ref, tri_ref, o_ref, carry_scr, acc_scr, *, tq):
    i = pl.program_id(1)
    tk = SB_CHUNK
    d = HEAD_DIM
    lane = lax.broadcasted_iota(jnp.int32, (tq, d), 1)
    q = jnp.concatenate([(q_ref[...].astype(F32) * ATTN_SCALE).astype(BF16),
                         jnp.where(lane < 2, 1.0, 0.0).astype(BF16)], axis=1)
    carry_scr[...] = jnp.zeros_like(carry_scr)
    acc_scr[...] = jnp.zeros_like(acc_scr)
    zeros_v = jnp.zeros((tk, d), BF16)

    def step(j, masked):
        k = k_ref[pl.ds(pl.multiple_of(j * tk, tk), tk), :]
        v = v_ref[pl.ds(pl.multiple_of(j * tk, tk), tk), :]
        z = _dot_nt(q, jnp.concatenate([k, b_ref[...]], axis=1))
        sp = _softplus(z)
        if masked:
            rows = i * tq + lax.broadcasted_iota(jnp.int32, (tq, tk), 0)
            cols = j * tk + lax.broadcasted_iota(jnp.int32, (tq, tk), 1)
            valid = cols < rows
            sp = jnp.where(valid, sp, 0.0)
        carry = carry_scr[...]
        r = _dot(sp.astype(BF16), tri_ref[...])
        a = jnp.exp(z - r - jnp.tile(carry, (1, tk // LANES)))
        if masked:
            a = jnp.where(valid, a, 0.0)
        carry_scr[...] = carry + r[:, 0:1]
        acc_scr[...] += _dot(a.astype(BF16), jnp.concatenate([v, zeros_v], axis=1))

    n_full = (i * tq) // tk
    for dgn in reversed(range(tq // tk)):
        step(n_full + dgn, True)

    _unrolled_loops(n_full, lambda jj: step(n_full - 1 - jj, False))
    o_ref[...] = _rms(acc_scr[:, :d], g_ref[...]).astype(o_ref.dtype)


def _sb_attn(zb, b_sb, g, col_q, col_k, col_v, tq=512):
    p = zb.shape[0]
    d = HEAD_DIM
    r = lax.broadcasted_iota(jnp.int32, (SB_CHUNK, SB_CHUNK), 0)
    c = lax.broadcasted_iota(jnp.int32, (SB_CHUNK, SB_CHUNK), 1)
    tri = (r >= c).astype(BF16)
    b_hi = b_sb.astype(BF16)
    b_lo = (b_sb - b_hi.astype(F32)).astype(BF16)
    lane = lax.broadcasted_iota(jnp.int32, (N_SB, SB_CHUNK, d), 2)
    bias = jnp.where(lane == 0, b_hi[:, None, None], jnp.where(lane == 1, b_lo[:, None, None], 0)).astype(BF16)
    return pl.pallas_call(
        functools.partial(_sb_kernel, tq=tq),
        grid=(N_SB, p // tq),
        in_specs=[pl.BlockSpec((tq, d), lambda h, i: (i, col_q + h)),
                  pl.BlockSpec((p, d), lambda h, i: (0, col_k + h)),
                  pl.BlockSpec((p, d), lambda h, i: (0, col_v + h)),
                  pl.BlockSpec((None, SB_CHUNK, d), lambda h, i: (h, 0, 0)),
                  pl.BlockSpec((None, 1, d), lambda h, i: (h, 0, 0)),
                  pl.BlockSpec((SB_CHUNK, SB_CHUNK), lambda h, i: (0, 0))],
        out_specs=pl.BlockSpec((tq, d), lambda h, i: (i, h)),
        out_shape=jax.ShapeDtypeStruct((p, N_SB * d), BF16),
        scratch_shapes=[pltpu.VMEM((tq, LANES), F32), pltpu.VMEM((tq, 2 * d), F32)],
        compiler_params=_cp(("arbitrary", "arbitrary")),
        name="sb_attn",
    )(zb, zb, zb, bias, g, tri)


def _mixout_kernel(of_ref, os_ref, x_ref, gt_ref, sc_ref, sh_ref, gpm_ref, gpf_ref, wo_ref, wr_ref, br_ref,
                   x1_ref, h2_ref, idx_ref, gate_ref, rank_ref, wd_ref, cnt_ref, carry_scr):
    i = pl.program_id(0)
    tm = x_ref.shape[0]
    half = of_ref.shape[1]

    @pl.when(i == 0)
    def _():
        carry_scr[...] = jnp.zeros_like(carry_scr)

    mix = (_dot(of_ref[...].astype(BF16), wo_ref[0:half, :])
           + _dot(os_ref[...].astype(BF16), wo_ref[half:2 * half, :]))
    x1 = x_ref[...] + gt_ref[...] * _rms(mix, gpm_ref[...])
    x1_ref[...] = x1
    h2 = _rms(x1, gpf_ref[...]) * (1.0 + sc_ref[...]) + sh_ref[...]
    h2_ref[...] = h2
    logits = _dot(h2.astype(BF16), wr_ref[...]) + br_ref[...]
    lane = lax.broadcasted_iota(jnp.int32, (tm, LANES), 1)
    lane_f = lane.astype(F32)
    vals = logits
    top_i, top_v = [], []
    for _ in range(TOP_K):
        mk = jnp.max(vals, axis=1, keepdims=True)
        ik = jnp.min(jnp.where(vals == mk, lane_f, float(LANES)), axis=1, keepdims=True)
        vals = jnp.where(lane_f == ik, -jnp.inf, vals)
        top_i.append(ik)
        top_v.append(mk)
    ex = [jnp.exp(v - top_v[0]) for v in top_v]
    den = ex[0] + ex[1] + ex[2] + ex[3]
    gates = [e / den for e in ex]
    hot = [lane_f == ik for ik in top_i]
    sel = jnp.zeros((tm, LANES), F32)
    wdense = jnp.zeros((tm, LANES), F32)
    for k in range(TOP_K):
        sel = sel + hot[k].astype(F32)
        wdense = wdense + jnp.where(hot[k], gates[k], 0.0)
    rows = lax.broadcasted_iota(jnp.int32, (tm, tm), 0)
    cols = lax.broadcasted_iota(jnp.int32, (tm, tm), 1)
    before = (cols < rows).astype(BF16)
    cnt = _dot(before, sel.astype(BF16)) + carry_scr[...]
    idx_o = jnp.zeros((tm, LANES), F32)
    gate_o = jnp.zeros((tm, LANES), F32)
    rank_o = jnp.zeros((tm, LANES), F32)
    for k in range(TOP_K):
        rk = jnp.sum(jnp.where(hot[k], cnt, 0.0), axis=1, keepdims=True)
        idx_o = jnp.where(lane == k, top_i[k], idx_o)
        gate_o = jnp.where(lane == k, gates[k], gate_o)
        rank_o = jnp.where(lane == k, rk, rank_o)
    idx_ref[...] = idx_o.astype(jnp.int32)
    gate_ref[...] = gate_o
    rank_ref[...] = rank_o.astype(jnp.int32)
    wd_ref[...] = wdense
    total = carry_scr[...] + jnp.sum(sel, axis=0, keepdims=True)
    carry_scr[...] = total
    cnt_ref[...] = total


def _mixout(of, os_, x, gt, sc, sh, gpm, gpf, wo, wr, br, tm):
    p, d = x.shape
    half = of.shape[1]
    row = lambda i: (i, 0)
    const = lambda i: (0, 0)
    mrow = row if gt.shape[0] == p else const
    msh = (tm, d) if gt.shape[0] == p else (1, d)
    return pl.pallas_call(
        _mixout_kernel,
        grid=(p // tm,),
        in_specs=[pl.BlockSpec((tm, half), row), pl.BlockSpec((tm, half), row), pl.BlockSpec((tm, d), row),
                  pl.BlockSpec(msh, mrow), pl.BlockSpec(msh, mrow), pl.BlockSpec(msh, mrow),
                  pl.BlockSpec((1, d), const), pl.BlockSpec((1, d), const),
                  pl.BlockSpec((2 * half, d), const),
                  pl.BlockSpec((d, LANES), const), pl.BlockSpec((1, LANES), const)],
        out_specs=[pl.BlockSpec((tm, d), row), pl.BlockSpec((tm, d), row),
                   pl.BlockSpec((tm, LANES), row), pl.BlockSpec((tm, LANES), row),
                   pl.BlockSpec((tm, LANES), row), pl.BlockSpec((tm, LANES), row),
                   pl.BlockSpec((1, LANES), const)],
        out_shape=[jax.ShapeDtypeStruct((p, d), F32), jax.ShapeDtypeStruct((p, d), F32),
                   jax.ShapeDtypeStruct((p, LANES), jnp.int32), jax.ShapeDtypeStruct((p, LANES), F32),
                   jax.ShapeDtypeStruct((p, LANES), jnp.int32), jax.ShapeDtypeStruct((p, LANES), F32),
                   jax.ShapeDtypeStruct((1, LANES), F32)],
        scratch_shapes=[pltpu.VMEM((1, LANES), F32)],
        compiler_params=_cp(("arbitrary",)),
        name="mixout_router",
    )(of, os_, x, gt, sc, sh, gpm, gpf, wo, wr, br)


def _dispatch_kernel(dest_ref, h_ref, xs_in, xs_out, sem, *, tt):
    del xs_in

    def copy(t, k):
        d = dest_ref[t * TOP_K + k]
        return pltpu.make_async_copy(h_ref.at[pl.ds(t, 1)], xs_out.at[pl.ds(d, 1)], sem)

    def start(t, c):
        for k in range(TOP_K):
            copy(t, k).start()
        return c

    def wait(t, c):
        for k in range(TOP_K):
            copy(t, k).wait()
        return c

    lax.fori_loop(0, tt, start, 0)
    lax.fori_loop(0, tt, wait, 0)


def _dispatch(dest_flat, h2, xs_zero, tt=128):
    p, d = h2.shape
    return pl.pallas_call(
        functools.partial(_dispatch_kernel, tt=tt),
        grid=(p // tt,),
        in_specs=[pl.BlockSpec((tt * TOP_K,), lambda i: (i,), memory_space=pltpu.SMEM),
                  pl.BlockSpec((tt, d), lambda i: (i, 0)),
                  pl.BlockSpec(memory_space=pl.ANY)],
        out_specs=pl.BlockSpec(memory_space=pl.ANY),
        out_shape=jax.ShapeDtypeStruct(xs_zero.shape, xs_zero.dtype),
        scratch_shapes=[pltpu.SemaphoreType.DMA(())],
        input_output_aliases={2: 0},
        compiler_params=_cp(("arbitrary",)),
        name="moe_dispatch",
    )(dest_flat, h2, xs_zero)


def _gate_up_kernel(te_ref, nu_ref, x_ref, wg_ref, wu_ref, bg_ref, bu_ref, h_ref, wgb, wub):
    i = pl.program_id(1)

    @pl.when(i < nu_ref[0])
    def _():
        prev = te_ref[jnp.maximum(i - 1, 0)]
        fresh = jnp.logical_or(i == 0, te_ref[i] != prev)

        @pl.when(fresh)
        def _():
            wgb[...] = wg_ref[...].astype(BF16)
            wub[...] = wu_ref[...].astype(BF16)

        x = x_ref[...].astype(BF16)
        g = jnp.minimum(_dot(x, wgb[...]) + bg_ref[...], SWIGLU_LIMIT)
        u = jnp.clip(_dot(x, wub[...]) + bu_ref[...], -SWIGLU_LIMIT, SWIGLU_LIMIT)
        act = g * (1.0 / (1.0 + jnp.exp(-SWIGLU_ALPHA * g))) * (u + 1.0)
        h_ref[...] = act.astype(h_ref.dtype)

    @pl.when(i >= nu_ref[0])
    def _():
        h_ref[...] = jnp.zeros_like(h_ref)


def _gate_up(tile_e, n_used, xs, w_gu, b_gu, tm, tn=512):
    r, d = xs.shape
    f = w_gu.shape[2] // 2
    nb = f // tn

    def tile(n, i, te, nu):
        return jnp.minimum(i, nu[0] - 1)

    return pl.pallas_call(
        _gate_up_kernel,
        grid_spec=pltpu.PrefetchScalarGridSpec(
            num_scalar_prefetch=2,
            grid=(nb, r // tm),
            in_specs=[pl.BlockSpec((tm, d), lambda n, i, te, nu: (tile(n, i, te, nu), 0)),
                      pl.BlockSpec((None, d, tn), lambda n, i, te, nu: (te[tile(n, i, te, nu)], 0, n)),
                      pl.BlockSpec((None, d, tn), lambda n, i, te, nu: (te[tile(n, i, te, nu)], 0, nb + n)),
                      pl.BlockSpec((None, 1, tn), lambda n, i, te, nu: (te[tile(n, i, te, nu)], 0, n)),
                      pl.BlockSpec((None, 1, tn), lambda n, i, te, nu: (te[tile(n, i, te, nu)], 0, nb + n))],
            out_specs=pl.BlockSpec((tm, tn), lambda n, i, te, nu: (i, n)),
            scratch_shapes=[pltpu.VMEM((d, tn), BF16), pltpu.VMEM((d, tn), BF16)]),
        out_shape=jax.ShapeDtypeStruct((r, f), BF16),
        compiler_params=_cp(("arbitrary", "arbitrary")),
        name="moe_gate_up",
    )(tile_e, n_used, xs, w_gu, w_gu, b_gu, b_gu)


def _down_kernel(te_ref, nu_ref, h_ref, w_ref, b_ref, y_ref, wb):
    i = pl.program_id(1)

    @pl.when(i < nu_ref[0])
    def _():
        prev = te_ref[jnp.maximum(i - 1, 0)]
        fresh = jnp.logical_or(i == 0, te_ref[i] != prev)

        @pl.when(fresh)
        def _():
            wb[...] = w_ref[...].astype(BF16)

        y_ref[...] = _dot(h_ref[...], wb[...]) + b_ref[...]

    @pl.when(i >= nu_ref[0])
    def _():
        y_ref[...] = jnp.zeros_like(y_ref)


def _down(tile_e, n_used, hs, w_dn, b_dn, tm, tn=1024):
    r, f = hs.shape
    d = w_dn.shape[2]

    def tile(n, i, te, nu):
        return jnp.minimum(i, nu[0] - 1)

    return pl.pallas_call(
        _down_kernel,
        grid_spec=pltpu.PrefetchScalarGridSpec(
            num_scalar_prefetch=2,
            grid=(d // tn, r // tm),
            in_specs=[pl.BlockSpec((tm, f), lambda n, i, te, nu: (tile(n, i, te, nu), 0)),
                      pl.BlockSpec((None, f, tn), lambda n, i, te, nu: (te[tile(n, i, te, nu)], 0, n)),
                      pl.BlockSpec((None, 1, tn), lambda n, i, te, nu: (te[tile(n, i, te, nu)], 0, n))],
            out_specs=pl.BlockSpec((tm, tn), lambda n, i, te, nu: (i, n)),
            scratch_shapes=[pltpu.VMEM((f, tn), BF16)]),
        out_shape=jax.ShapeDtypeStruct((r, d), F32),
        compiler_params=_cp(("arbitrary", "arbitrary")),
        name="moe_down",
    )(tile_e, n_used, hs, w_dn, b_dn)


def _combine_kernel(dest_ref, y_hbm, gate_ref, x1_ref, gt_ref, g_ref, o_ref, buf, sem, *, tt):
    def copy(t, k):
        d = dest_ref[t * TOP_K + k]
        return pltpu.make_async_copy(y_hbm.at[pl.ds(d, 1)], buf.at[k, pl.ds(t, 1)], sem)

    def start(t, c):
        for k in range(TOP_K):
            copy(t, k).start()
        return c

    def wait(t, c):
        for k in range(TOP_K):
            copy(t, k).wait()
        return c

    lax.fori_loop(0, tt, start, 0)
    lax.fori_loop(0, tt, wait, 0)
    gate = gate_ref[...]
    ffn = gate[:, 0:1] * buf[0]
    for k in range(1, TOP_K):
        ffn = ffn + gate[:, k:k + 1] * buf[k]
    o_ref[...] = x1_ref[...] + gt_ref[...] * _rms(ffn, g_ref[...])


def _combine(dest_flat, ys, gates, x1, gt, g, tt=128):
    p, d = x1.shape
    row = lambda i: (i, 0)
    const = lambda i: (0, 0)
    return pl.pallas_call(
        functools.partial(_combine_kernel, tt=tt),
        grid=(p // tt,),
        in_specs=[pl.BlockSpec((tt * TOP_K,), lambda i: (i,), memory_space=pltpu.SMEM),
                  pl.BlockSpec(memory_space=pl.ANY),
                  pl.BlockSpec((tt, LANES), row), pl.BlockSpec((tt, d), row),
                  pl.BlockSpec((1, d), const), pl.BlockSpec((1, d), const)],
        out_specs=pl.BlockSpec((tt, d), row),
        out_shape=jax.ShapeDtypeStruct((p, d), F32),
        scratch_shapes=[pltpu.VMEM((TOP_K, tt, d), F32), pltpu.SemaphoreType.DMA(())],
        compiler_params=_cp(("arbitrary",)),
        name="moe_combine",
    )(dest_flat, ys, gates, x1, gt, g)


def _moe_prompt(h2, top_idx, gates, rank, counts, x1, gt_m, g_post, w_gu, b_gu, w_dn, b_dn, tm=512):
    p, d = h2.shape
    n_tiles = (p * TOP_K) // tm + N_EXPERTS
    padded = (counts + tm - 1) // tm * tm
    pends = jnp.cumsum(padded)
    pstart = pends - padded
    dest = (pstart[top_idx] + rank).reshape(-1).astype(jnp.int32)
    bstart = jnp.arange(n_tiles, dtype=jnp.int32) * tm
    tile_e = jnp.minimum(jnp.sum(pends[None, :] <= bstart[:, None], axis=1), N_EXPERTS - 1).astype(jnp.int32)
    n_used = (pends[-1:] // tm).astype(jnp.int32)
    xs = _dispatch(dest, h2, jnp.zeros((n_tiles * tm, d), F32))
    hs = _gate_up(tile_e, n_used, xs, w_gu, b_gu[:, None, :], tm)
    ys = _down(tile_e, n_used, hs, w_dn, b_dn[:, None, :], tm)
    return _combine(dest, ys, gates, x1, gt_m, g_post)


def _inproj_sample_kernel(x_ref, sc_ref, sh_ref, g_ref, w_ref, bf_ref, z_ref, lf_ref, h_scr, *, nk, tk, off_f):
    k = pl.program_id(0)

    @pl.when(k == 0)
    def _():
        h = _rms(x_ref[...], g_ref[...]) * (1.0 + sc_ref[...]) + sh_ref[...]
        for kk in range(nk):
            h_scr[kk] = h[:, kk * tk:(kk + 1) * tk]
        z_ref[...] = jnp.zeros_like(z_ref)

    z_ref[...] += _dot(h_scr[k].astype(BF16), w_ref[...].astype(BF16))

    @pl.when(k == nk - 1)
    def _():
        lf_ref[...] = _log_sigmoid(z_ref[:, off_f:off_f + LANES] + bf_ref[...])


def _inproj_sample(x, sc, sh, g, w, bfor, off_f, tk=256):
    n, d = x.shape
    dn = w.shape[1]
    nk = d // tk
    const = lambda k: (0, 0)
    return pl.pallas_call(
        functools.partial(_inproj_sample_kernel, nk=nk, tk=tk, off_f=off_f),
        grid=(nk,),
        in_specs=[pl.BlockSpec((n, d), const), pl.BlockSpec((n, d), const), pl.BlockSpec((n, d), const),
                  pl.BlockSpec((1, d), const), pl.BlockSpec((tk, dn), lambda k: (k, 0)),
                  pl.BlockSpec((1, LANES), const)],
        out_specs=[pl.BlockSpec((n, dn), const), pl.BlockSpec((n, LANES), const)],
        out_shape=[jax.ShapeDtypeStruct((n, dn), F32), jax.ShapeDtypeStruct((n, LANES), F32)],
        scratch_shapes=[pltpu.VMEM((nk, n, tk), F32)],
        compiler_params=_cp(("arbitrary",)),
        name="inproj_sample",
    )(x, sc, sh, g, w, bfor)


def _decode_kernel(pt_ref, fq_ref, fkn_ref, fvn_ref, lfn_ref, sq_ref, bsb_ref, gf_ref, gs_ref,
                   fk_hbm, fv_hbm, lf_hbm, sk_hbm, sv_hbm, of_ref, os_ref,
                   buf, lfbuf, sems, lsem, m_scr, l_scr, acc_scr, dsuf_scr, sacc_scr, skeep_scr, *, n_pages):
    n = pl.program_id(0)
    grp = buf.shape[1]
    page = buf.shape[3]
    nh = fq_ref.shape[0]
    d = HEAD_DIM
    n_groups = n_pages // grp
    caches = (fk_hbm, fv_hbm, sk_hbm, sv_hbm)

    def fetch(seq, gi, slot):
        cps = []
        for g in range(grp):
            pg = pt_ref[seq, n_pages - 1 - gi * grp - g]
            cps += [pltpu.make_async_copy(hbm.at[pg, :, h, :], buf.at[slot, g, c, :, pl.ds(h * d, d)],
                                          sems.at[slot, c])
                    for c, hbm in enumerate(caches) for h in range(nh)]
            cps.append(pltpu.make_async_copy(lf_hbm.at[pg], lfbuf.at[slot, g], lsem.at[slot]))
        return cps

    first = n * n_groups

    @pl.when(n == 0)
    def _():
        for cp in fetch(0, 0, 0):
            cp.start()

    sub = lax.broadcasted_iota(jnp.int32, (nh, d), 0)
    sub_w = lax.broadcasted_iota(jnp.int32, (nh, nh * d), 0)
    blk_w = lax.broadcasted_iota(jnp.int32, (nh, nh * d), 1) // d

    def block_diag(q):
        return jnp.where(sub_w == blk_w, jnp.concatenate([q] * nh, axis=1), jnp.zeros((), q.dtype))

    def diag_blocks(o_full):
        o = jnp.zeros((nh, d), F32)
        for h in range(nh):
            o = jnp.where(sub == h, o_full[:, h * d:(h + 1) * d], o)
        return o

    q = fq_ref[...].astype(BF16)
    qf_bd = block_diag(q)
    qs_bd = block_diag(sq_ref[...].astype(BF16))
    kn = fkn_ref[...].astype(BF16).astype(F32)
    m_scr[...] = jnp.sum(q.astype(F32) * kn, axis=1, keepdims=True) * ATTN_SCALE
    l_scr[...] = jnp.ones_like(l_scr)
    acc_scr[...] = fvn_ref[...].astype(BF16).astype(F32)
    dsuf_scr[...] = jnp.zeros_like(dsuf_scr)
    sacc_scr[...] = jnp.zeros_like(sacc_scr)
    skeep_scr[...] = jnp.zeros_like(skeep_scr)

    rr = lax.broadcasted_iota(jnp.int32, (2 * page, page), 0) % page
    cc = lax.broadcasted_iota(jnp.int32, (2 * page, page), 1)
    later2 = (rr > cc).astype(BF16)

    def later_sums(x):
        hi = x.astype(BF16)
        lo = (x - hi.astype(F32)).astype(BF16)
        return _dot(jnp.concatenate([hi, lo], axis=1), later2)

    def body(gi, carry):
        slot = (first + gi) % 2

        @pl.when(gi + 1 < n_groups)
        def _():
            for cp in fetch(n, gi + 1, 1 - slot):
                cp.start()

        @pl.when(jnp.logical_and(gi + 1 == n_groups, n + 1 < pl.num_programs(0)))
        def _():
            for cp in fetch(n + 1, 0, 1 - slot):
                cp.start()

        for cp in fetch(n, gi, slot):
            cp.wait()

        dsuf = dsuf_scr[...]
        scs, pvs = [], None
        for g in range(grp):
            lf = lfbuf[slot, g]
            s = _dot_nt(qf_bd, buf[slot, g, 0].astype(BF16)) * ATTN_SCALE
            scs.append(s + lfn_ref[...] + later_sums(lf) + dsuf)
            dsuf = dsuf + jnp.sum(lf, axis=1, keepdims=True)
        dsuf_scr[...] = dsuf
        sc = jnp.concatenate(scs, axis=1)
        m_prev = m_scr[...]
        m_new = jnp.maximum(m_prev, jnp.max(sc, axis=1, keepdims=True))
        corr = jnp.exp(m_prev - m_new)
        pr = jnp.exp(sc - m_new)
        l_scr[...] = l_scr[...] * corr + jnp.sum(pr, axis=1, keepdims=True)
        for g in range(grp):
            o = _dot(pr[:, g * page:(g + 1) * page].astype(BF16), buf[slot, g, 1].astype(BF16))
            pvs = o if pvs is None else pvs + o
        acc_scr[...] = acc_scr[...] * corr + diag_blocks(pvs)
        m_scr[...] = m_new

        keep = skeep_scr[...]
        avs = None
        for g in range(grp):
            z = _dot_nt(qs_bd, buf[slot, g, 2].astype(BF16)) * ATTN_SCALE + bsb_ref[...]
            sp = _softplus(z)
            a = jnp.exp(z - sp - later_sums(sp) - keep)
            keep = keep + jnp.sum(sp, axis=1, keepdims=True)
            o = _dot(a.astype(BF16), buf[slot, g, 3].astype(BF16))
            avs = o if avs is None else avs + o
        skeep_scr[...] = keep
        sacc_scr[...] += diag_blocks(avs)
        return carry

    lax.fori_loop(0, n_groups, body, 0)
    of_ref[...] = _rms(acc_scr[...] / l_scr[...], gf_ref[...])
    os_ref[...] = _rms(sacc_scr[...], gs_ref[...])


def _decode(page_table, fq, fkn, fvn, lfn, sq, bsb, gf, gs, ck, cv, clf_t, csk, csv):
    n, n_pages = page_table.shape
    grp = math.gcd(n_pages, 8)
    nh, d = fq.shape[1], fq.shape[2]
    page = ck.shape[1]
    per_seq = lambda i, pt: (i, 0, 0)
    const = lambda i, pt: (0, 0)
    vec = pl.BlockSpec((None, nh, d), per_seq)
    col = pl.BlockSpec((None, nh, 1), per_seq)
    hbm = pl.BlockSpec(memory_space=pl.ANY)
    return pl.pallas_call(
        functools.partial(_decode_kernel, n_pages=n_pages),
        grid_spec=pltpu.PrefetchScalarGridSpec(
            num_scalar_prefetch=1,
            grid=(n,),
            in_specs=[vec, vec, vec, col, vec,
                      pl.BlockSpec((nh, 1), const), pl.BlockSpec((nh, d), const), pl.BlockSpec((nh, d), const),
                      hbm, hbm, hbm, hbm, hbm],
            out_specs=[vec, vec],
            scratch_shapes=[pltpu.VMEM((2, grp, 4, page, nh * d), F32), pltpu.VMEM((2, grp, nh, page), F32),
                            pltpu.SemaphoreType.DMA((2, 4)), pltpu.SemaphoreType.DMA((2,)),
                            pltpu.VMEM((nh, 1), F32), pltpu.VMEM((nh, 1), F32), pltpu.VMEM((nh, d), F32),
                            pltpu.VMEM((nh, 1), F32), pltpu.VMEM((nh, d), F32), pltpu.VMEM((nh, 1), F32)]),
        out_shape=[jax.ShapeDtypeStruct((n, nh, d), F32), jax.ShapeDtypeStruct((n, nh, d), F32)],
        compiler_params=_cp(("arbitrary",)),
        name="decode_attn",
    )(page_table, fq, fkn, fvn, lfn, sq, bsb, gf, gs, ck, cv, clf_t, csk, csv)


def _moe_small_kernel(h_ref, wd_ref, wg_ref, wu_ref, bg_ref, bu_ref, wdn_ref, bd_ref, x1_ref, gt_ref, g_ref,
                      y_ref, acc_scr, *, n_e, n_j):
    e = pl.program_id(0)
    j = pl.program_id(1)
    n = h_ref.shape[0]

    @pl.when(jnp.logical_and(e == 0, j == 0))
    def _():
        acc_scr[...] = jnp.zeros_like(acc_scr)

    lane = lax.broadcasted_iota(jnp.int32, (n, LANES), 1)
    we = jnp.sum(jnp.where(lane == e, wd_ref[...], 0.0), axis=1, keepdims=True)
    hb = h_ref[...].astype(BF16)
    g = jnp.minimum(_dot(hb, wg_ref[...].astype(BF16)) + bg_ref[...], SWIGLU_LIMIT)
    u = jnp.clip(_dot(hb, wu_ref[...].astype(BF16)) + bu_ref[...], -SWIGLU_LIMIT, SWIGLU_LIMIT)
    act = g * (1.0 / (1.0 + jnp.exp(-SWIGLU_ALPHA * g))) * (u + 1.0)
    y = _dot(act.astype(BF16), wdn_ref[...].astype(BF16))
    acc_scr[...] += we * y

    @pl.when(j == 0)
    def _():
        acc_scr[...] += we * bd_ref[...]

    @pl.when(jnp.logical_and(e == n_e - 1, j == n_j - 1))
    def _():
        y_ref[...] = x1_ref[...] + gt_ref[...] * _rms(acc_scr[...], g_ref[...])


def _moe_small(h2, wdense, w_gu, b_gu, w_dn, b_dn, x1, gt, g, tf=512):
    n, d = h2.shape
    n_e = w_gu.shape[0]
    f = w_gu.shape[2] // 2
    nj = f // tf
    const = lambda e, j: (0, 0)
    return pl.pallas_call(
        functools.partial(_moe_small_kernel, n_e=n_e, n_j=nj),
        grid=(n_e, nj),
        in_specs=[pl.BlockSpec((n, d), const), pl.BlockSpec((n, LANES), const),
                  pl.BlockSpec((None, d, tf), lambda e, j: (e, 0, j)),
                  pl.BlockSpec((None, d, tf), lambda e, j: (e, 0, nj + j)),
                  pl.BlockSpec((None, 1, tf), lambda e, j: (e, 0, j)),
                  pl.BlockSpec((None, 1, tf), lambda e, j: (e, 0, nj + j)),
                  pl.BlockSpec((None, tf, d), lambda e, j: (e, j, 0)),
                  pl.BlockSpec((None, 1, d), lambda e, j: (e, 0, 0)),
                  pl.BlockSpec((n, d), const), pl.BlockSpec((n, d), const), pl.BlockSpec((1, d), const)],
        out_specs=pl.BlockSpec((n, d), const),
        out_shape=jax.ShapeDtypeStruct((n, d), F32),
        scratch_shapes=[pltpu.VMEM((n, d), F32)],
        compiler_params=_cp(("arbitrary", "arbitrary")),
        name="moe_sample",
    )(h2, wdense, w_gu, w_gu, b_gu, b_gu, w_dn, b_dn, x1, gt, g)


def _pad_lanes(a, fill=0.0):
    return jnp.pad(a, ((0, 0), (0, LANES - a.shape[1])), constant_values=fill)


def kernel(x_prompt, x_sample, cache_fox_k, cache_fox_v, cache_fox_logf, cache_sb_k, cache_sb_v, page_table,
           c_prompt, c_sample, w_ada, b_ada, g_pre_mix, w_in, b_forget, b_sb, g_fox_out, g_sb_out, w_out,
           g_post_mix, g_pre_ffn, w_router, b_router, w_gate_up, b_gate_up, w_down, b_down, g_post_ffn):
    depth = w_ada.shape[0]
    assert depth == 1 and x_sample.shape[1] == 1 and x_prompt.shape[0] == 1
    bsz, seq, d = x_prompt.shape
    ns = x_sample.shape[0]
    p = bsz * seq
    dfox = N_FOX * HEAD_DIM
    dsb = N_SB * HEAD_DIM
    off_ff = 3 * dfox
    off_sq = off_ff + N_FOX
    xp = x_prompt.reshape(p, d)
    xs = x_sample.reshape(ns, d)

    n_c = bsz + ns
    c_all = jnp.concatenate([c_prompt, c_sample, jnp.zeros((-n_c % 8, d), F32)], axis=0)
    mod = _ada(c_all, w_ada[0], b_ada[0][None])
    sh_a, sc_a, gt_a, sh_m, sc_m, gt_m = [mod[0:bsz, k * d:(k + 1) * d] for k in range(6)]
    ssh_a, ssc_a, sgt_a, ssh_m, ssc_m, sgt_m = [mod[bsz:n_c, k * d:(k + 1) * d] for k in range(6)]

    w_in0 = w_in[0]
    bfor = _pad_lanes(b_forget[0][None])
    gpre = g_pre_mix[0][None]
    gf = g_fox_out[0]
    gs = g_sb_out[0]
    wr = _pad_lanes(w_router[0]).astype(BF16)
    wo = w_out[0].astype(BF16)
    br = _pad_lanes(b_router[0][None], NEG_BIG)
    gpm = g_post_mix[0][None]
    gpf = g_pre_ffn[0][None]
    gpo = g_post_ffn[0][None]

    w6 = jnp.concatenate([w_in0[:, :off_ff], w_in0[:, off_sq:]], axis=1).astype(BF16)
    wf = _pad_lanes(w_in0[:, off_ff:off_sq]).astype(BF16)
    z, zb, lf, fcum = _inproj_prompt(xp, sc_a, sh_a, gpre, w6, wf, bfor)
    tkf = 512
    nf2 = (-LOG2E * fcum[:, :N_FOX]).T.reshape(N_FOX, p // tkf, 1, tkf)
    hb = dfox // HEAD_DIM
    o_fox = _fox_attn(zb, nf2, gf[:, None, :], 0, hb, 2 * hb, tq=tkf)
    o_sb = _sb_attn(zb, b_sb[0], gs[:, None, :], 3 * hb, 4 * hb, 5 * hb)
    x1, h2, top_idx, gates, rank, _, counts = _mixout(
        o_fox, o_sb, xp, gt_a, sc_m, sh_m, gpm, gpf, wo, wr, br, tm=256)
    cnt = counts[0, :N_EXPERTS].astype(jnp.int32)
    y_p = _moe_prompt(h2, top_idx[:, :TOP_K], gates, rank[:, :TOP_K], cnt, x1, gt_m, gpo,
                      w_gate_up[0], b_gate_up[0], w_down[0], b_down[0])

    zs, lfs = _inproj_sample(xs, ssc_a, ssh_a, gpre, w_in0, bfor, off_ff)
    heads = lambda a: a.reshape(ns, -1, HEAD_DIM)
    fq_s, fk_s, fv_s = heads(zs[:, :dfox]), heads(zs[:, dfox:2 * dfox]), heads(zs[:, 2 * dfox:off_ff])
    sq_s = heads(zs[:, off_sq:off_sq + dsb])
    sk_s = heads(zs[:, off_sq + dsb:off_sq + 2 * dsb])
    sv_s = heads(zs[:, off_sq + 2 * dsb:])
    lf_s = lfs[:, :N_FOX]
    drop0 = lambda a: a.reshape(a.shape[1:])
    of_s, os_s = _decode(page_table, fq_s, fk_s, fv_s, lf_s[:, :, None], sq_s, b_sb[0][:, None], gf, gs,
                         drop0(cache_fox_k), drop0(cache_fox_v), jnp.swapaxes(drop0(cache_fox_logf), 1, 2),
                         drop0(cache_sb_k), drop0(cache_sb_v))
    x1s, h2s, _, _, _, wdense, _ = _mixout(
        of_s.reshape(ns, dfox), os_s.reshape(ns, dsb), xs, sgt_a, ssc_m, ssh_m, gpm, gpf, wo, wr, br, tm=ns)
    y_s = _moe_small(h2s, wdense, w_gate_up[0], b_gate_up[0][:, None, :], w_down[0], b_down[0][:, None, :],
                     x1s, sgt_m, gpo)

    rows5 = lambda a, nh: a.reshape(1, bsz, seq, nh, HEAD_DIM)
    srows5 = lambda a: a.reshape(1, ns, 1, -1, HEAD_DIM)
    return (y_p.reshape(bsz, seq, d), y_s.reshape(ns, 1, d),
            rows5(z[:, dfox:2 * dfox], N_FOX), rows5(z[:, 2 * dfox:3 * dfox], N_FOX),
            lf[:, :N_FOX].reshape(1, bsz, seq, N_FOX),
            rows5(z[:, 3 * dfox + dsb:3 * dfox + 2 * dsb], N_SB), rows5(z[:, 3 * dfox + 2 * dsb:], N_SB),
            srows5(fk_s), srows5(fv_s), lf_s.reshape(1, ns, 1, N_FOX), srows5(sk_s), srows5(sv_s))
```

```python
import functools
import math

import jax
import jax.numpy as jnp
from jax import lax
from jax.experimental import pallas as pl
from jax.experimental.pallas import tpu as pltpu

F32 = jnp.float32
BF16 = jnp.bfloat16
HI = lax.Precision.HIGHEST

HEAD_DIM = 128
N_FOX = 8
N_SB = 8
TOP_K = 4
N_EXPERTS = 32
LANES = 128
NORM_EPS = 1e-6
SWIGLU_ALPHA = 1.702
SWIGLU_LIMIT = 7.0
ATTN_SCALE = HEAD_DIM ** -0.5
LOG2E = math.log2(math.e)
VMEM_LIMIT = 56 * 1024 * 1024
NEG_BIG = -1e30


def _cp(sem, vmem=VMEM_LIMIT, flags=None):
    return pltpu.CompilerParams(dimension_semantics=sem, vmem_limit_bytes=vmem, flags=flags)


def _rms(x, g):
    return x * lax.rsqrt(jnp.mean(x * x, axis=-1, keepdims=True) + NORM_EPS) * g


def _log_sigmoid(x):
    return jnp.minimum(x, 0.0) - jnp.log(1.0 + jnp.exp(-jnp.abs(x)))


def _softplus(x):
    return jnp.maximum(x, 0.0) + jnp.log(1.0 + jnp.exp2(jnp.abs(x) * (-LOG2E)))


def _dot(a, b, precision=None):
    return jnp.dot(a, b, preferred_element_type=F32, precision=precision)


def _dot_nt(a, b):
    return lax.dot_general(a, b, (((1,), (1,)), ((), ())), preferred_element_type=F32)


def _ada_kernel(c_ref, w_ref, b_ref, o_ref):
    c = c_ref[...]
    s = c / (1.0 + jnp.exp(-c))
    o_ref[...] = _dot(s.astype(BF16), w_ref[...].astype(BF16)) + b_ref[...]


def _ada(c_all, w, b, tn=1024):
    r, d = c_all.shape
    n = w.shape[1]
    return pl.pallas_call(
        _ada_kernel,
        grid=(n // tn,),
        in_specs=[pl.BlockSpec((r, d), lambda j: (0, 0)),
                  pl.BlockSpec((d, tn), lambda j: (0, j)),
                  pl.BlockSpec((1, tn), lambda j: (0, j))],
        out_specs=pl.BlockSpec((r, tn), lambda j: (0, j)),
        out_shape=jax.ShapeDtypeStruct((r, n), F32),
        compiler_params=_cp(("arbitrary",)),
        name="ada_mod",
    )(c_all, w, b)


def _inproj_kernel(x_ref, sc_ref, sh_ref, g_ref, w_ref, wf_ref, bf_ref,
                   z_ref, zb_ref, lf_ref, fc_ref, h_scr, carry_scr):
    i = pl.program_id(0)
    j = pl.program_id(1)
    tm = x_ref.shape[0]

    @pl.when(j == 0)
    def _():
        h = _rms(x_ref[...], g_ref[...]) * (1.0 + sc_ref[...]) + sh_ref[...]
        hb = h.astype(BF16)
        h_scr[...] = hb
        lf = _log_sigmoid(_dot(hb, wf_ref[...]) + bf_ref[...])
        lf_ref[...] = lf

        @pl.when(i == 0)
        def _():
            carry_scr[...] = jnp.zeros_like(carry_scr)

        rows = lax.broadcasted_iota(jnp.int32, (tm, tm), 0)
        cols = lax.broadcasted_iota(jnp.int32, (tm, tm), 1)
        tri = (cols <= rows).astype(F32)
        csum = _dot(tri, lf, HI) + carry_scr[...]
        fc_ref[...] = csum
        carry_scr[...] = csum[tm - 1:tm, :]

    z = _dot(h_scr[...], w_ref[...])
    z_ref[...] = z
    zb_ref[...] = z.astype(BF16)


def _inproj_prompt(x, sc, sh, g, w6, wf, bfor, tm=512, tn=1024):
    p, d = x.shape
    n = w6.shape[1]
    row = lambda i, j: (i, 0)
    const = lambda i, j: (0, 0)
    return pl.pallas_call(
        _inproj_kernel,
        grid=(p // tm, n // tn),
        in_specs=[pl.BlockSpec((tm, d), row),
                  pl.BlockSpec((1, d), const), pl.BlockSpec((1, d), const), pl.BlockSpec((1, d), const),
                  pl.BlockSpec((d, tn), lambda i, j: (0, j)),
                  pl.BlockSpec((d, LANES), const), pl.BlockSpec((1, LANES), const)],
        out_specs=[pl.BlockSpec((tm, tn), lambda i, j: (i, j)),
                   pl.BlockSpec((tm, tn), lambda i, j: (i, j)),
                   pl.BlockSpec((tm, LANES), row),
                   pl.BlockSpec((tm, LANES), row)],
        out_shape=[jax.ShapeDtypeStruct((p, n), F32), jax.ShapeDtypeStruct((p, n), BF16),
                   jax.ShapeDtypeStruct((p, LANES), F32), jax.ShapeDtypeStruct((p, LANES), F32)],
        scratch_shapes=[pltpu.VMEM((tm, d), BF16), pltpu.VMEM((1, LANES), F32)],
        compiler_params=_cp(("arbitrary", "arbitrary")),
        name="inproj_prompt",
    )(x, sc, sh, g, w6, wf, bfor)


CHUNK_UNROLLS = (8, 2, 1)


def _unrolled_loops(n, step):
    done = 0
    for u in CHUNK_UNROLLS:
        trips = (n - done) // u

        def body(g, c, u=u, done=done):
            for k in range(u):
                step(done + g * u + k)
            return c

        lax.fori_loop(0, trips, body, 0)
        done = done + trips * u


def _fox_kernel(q_ref, k_ref, v_ref, nf_ref, g_ref, o_ref, m_scr, acc_scr, *, tq):
    i = pl.program_id(1)
    d = HEAD_DIM
    tk = tq
    q = (q_ref[...].astype(F32) * (ATTN_SCALE * LOG2E)).astype(BF16)
    m_scr[...] = jnp.full_like(m_scr, -jnp.inf)
    acc_scr[...] = jnp.zeros_like(acc_scr)
    ones = jnp.ones((tk, d), BF16)

    def update(j, masked):
        k = k_ref[pl.ds(pl.multiple_of(j * tk, tk), tk), :]
        v = v_ref[pl.ds(pl.multiple_of(j * tk, tk), tk), :]
        s = _dot_nt(q, k) + nf_ref[j]
        if masked:
            rows = lax.broadcasted_iota(jnp.int32, (tq, tk), 0)
            cols = lax.broadcasted_iota(jnp.int32, (tq, tk), 1)
            s = jnp.where(cols <= rows, s, -jnp.inf)
        m_prev = m_scr[...]
        m_new = jnp.maximum(m_prev, jnp.max(s, axis=1, keepdims=True))
        alpha = jnp.exp2(m_prev - m_new)
        p = jnp.exp2(s - jnp.tile(m_new, (1, tk // LANES)))
        pv = _dot(p.astype(BF16), jnp.concatenate([v, ones], axis=1))
        acc_scr[...] = jnp.tile(alpha, (1, 2 * d // LANES)) * acc_scr[...] + pv
        m_scr[...] = m_new

    _unrolled_loops(i, lambda j: update(j, False))
    update(i, True)
    o = acc_scr[:, :d] / acc_scr[:, d:]
    o_ref[...] = _rms(o, g_ref[...]).astype(o_ref.dtype)


def _fox_attn(zb, nf2, g, col_q, col_k, col_v, tq):
    p = zb.shape[0]
    d = HEAD_DIM
    tk = tq
    return pl.pallas_call(
        functools.partial(_fox_kernel, tq=tq),
        grid=(N_FOX, p // tq),
        in_specs=[pl.BlockSpec((tq, d), lambda h, i: (i, col_q + h)),
                  pl.BlockSpec((p, d), lambda h, i: (0, col_k + h)),
                  pl.BlockSpec((p, d), lambda h, i: (0, col_v + h)),
                  pl.BlockSpec((None, p // tk, 1, tk), lambda h, i: (h, 0, 0, 0)),
                  pl.BlockSpec((None, 1, d), lambda h, i: (h, 0, 0))],
        out_specs=pl.BlockSpec((tq, d), lambda h, i: (i, h)),
        out_shape=jax.ShapeDtypeStruct((p, N_FOX * d), BF16),
        scratch_shapes=[pltpu.VMEM((tq, LANES), F32), pltpu.VMEM((tq, 2 * d), F32)],
        compiler_params=_cp(("arbitrary", "arbitrary")),
        name="fox_attn",
    )(zb, zb, zb, nf2, g)


SB_CHUNK = 2 * LANES


def _sb_kernel(q_ref, k_ref, v_ref, b_ref, g_ref, tri_ref, o_ref, carry_scr, acc_scr, *, tq):
    i = pl.program_id(1)
    tk = SB_CHUNK
    d = HEAD_DIM
    lane = lax.broadcasted_iota(jnp.int32, (tq, d), 1)
    q = jnp.concatenate([(q_ref[...].astype(F32) * ATTN_SCALE).astype(BF16),
                         jnp.where(lane < 2, 1.0, 0.0).astype(BF16)], axis=1)
    carry_scr[...] = jnp.zeros_like(carry_scr)
    acc_scr[...] = jnp.zeros_like(acc_scr)
    zeros_v = jnp.zeros((tk, d), BF16)

    def step(j, masked):
        k = k_ref[pl.ds(pl.multiple_of(j * tk, tk), tk), :]
        v = v_ref[pl.ds(pl.multiple_of(j * tk, tk), tk), :]
        z = _dot_nt(q, jnp.concatenate([k, b_ref[...]], axis=1))
        sp = _softplus(z)
        if masked:
            rows = i * tq + lax.broadcasted_iota(jnp.int32, (tq, tk), 0)
            cols = j * tk + lax.broadcasted_iota(jnp.int32, (tq, tk), 1)
            valid = cols < rows
            sp = jnp.where(valid, sp, 0.0)
        carry = carry_scr[...]
        r = _dot(sp.astype(BF16), tri_ref[...])
        a = jnp.exp(z - r - jnp.tile(carry, (1, tk // LANES)))
        if masked:
            a = jnp.where(valid, a, 0.0)
        carry_scr[...] = carry + r[:, 0:1]
        acc_scr[...] += _dot(a.astype(BF16), jnp.concatenate([v, zeros_v], axis=1))

    n_full = (i * tq) // tk
    for dgn in reversed(range(tq // tk)):
        step(n_full + dgn, True)

    _unrolled_loops(n_full, lambda jj: step(n_full - 1 - jj, False))
    o_ref[...] = _rms(acc_scr[:, :d], g_ref[...]).astype(o_ref.dtype)


def _sb_attn(zb, b_sb, g, col_q, col_k, col_v, tq=512):
    p = zb.shape[0]
    d = HEAD_DIM
    r = lax.broadcasted_iota(jnp.int32, (SB_CHUNK, SB_CHUNK), 0)
    c = lax.broadcasted_iota(jnp.int32, (SB_CHUNK, SB_CHUNK), 1)
    tri = (r >= c).astype(BF16)
    b_hi = b_sb.astype(BF16)
    b_lo = (b_sb - b_hi.astype(F32)).astype(BF16)
    lane = lax.broadcasted_iota(jnp.int32, (N_SB, SB_CHUNK, d), 2)
    bias = jnp.where(lane == 0, b_hi[:, None, None], jnp.where(lane == 1, b_lo[:, None, None], 0)).astype(BF16)
    return pl.pallas_call(
        functools.partial(_sb_kernel, tq=tq),
        grid=(N_SB, p // tq),
        in_specs=[pl.BlockSpec((tq, d), lambda h, i: (i, col_q + h)),
                  pl.BlockSpec((p, d), lambda h, i: (0, col_k + h)),
                  pl.BlockSpec((p, d), lambda h, i: (0, col_v + h)),
                  pl.BlockSpec((None, SB_CHUNK, d), lambda h, i: (h, 0, 0)),
                  pl.BlockSpec((None, 1, d), lambda h, i: (h, 0, 0)),
                  pl.BlockSpec((SB_CHUNK, SB_CHUNK), lambda h, i: (0, 0))],
        out_specs=pl.BlockSpec((tq, d), lambda h, i: (i, h)),
        out_shape=jax.ShapeDtypeStruct((p, N_SB * d), BF16),
        scratch_shapes=[pltpu.VMEM((tq, LANES), F32), pltpu.VMEM((tq, 2 * d), F32)],
        compiler_params=_cp(("arbitrary", "arbitrary")),
        name="sb_attn",
    )(zb, zb, zb, bias, g, tri)


def _mixout_kernel(of_ref, os_ref, x_ref, gt_ref, sc_ref, sh_ref, gpm_ref, gpf_ref, wo_ref, wr_ref, br_ref,
                   x1_ref, h2_ref, idx_ref, gate_ref, rank_ref, wd_ref, cnt_ref, carry_scr):
    i = pl.program_id(0)
    tm = x_ref.shape[0]
    half = of_ref.shape[1]

    @pl.when(i == 0)
    def _():
        carry_scr[...] = jnp.zeros_like(carry_scr)

    mix = (_dot(of_ref[...].astype(BF16), wo_ref[0:half, :])
           + _dot(os_ref[...].astype(BF16), wo_ref[half:2 * half, :]))
    x1 = x_ref[...] + gt_ref[...] * _rms(mix, gpm_ref[...])
    x1_ref[...] = x1
    h2 = _rms(x1, gpf_ref[...]) * (1.0 + sc_ref[...]) + sh_ref[...]
    h2_ref[...] = h2
    logits = _dot(h2.astype(BF16), wr_ref[...]) + br_ref[...]
    lane = lax.broadcasted_iota(jnp.int32, (tm, LANES), 1)
    lane_f = lane.astype(F32)
    vals = logits
    top_i, top_v = [], []
    for _ in range(TOP_K):
        mk = jnp.max(vals, axis=1, keepdims=True)
        ik = jnp.min(jnp.where(vals == mk, lane_f, float(LANES)), axis=1, keepdims=True)
        vals = jnp.where(lane_f == ik, -jnp.inf, vals)
        top_i.append(ik)
        top_v.append(mk)
    ex = [jnp.exp(v - top_v[0]) for v in top_v]
    den = ex[0] + ex[1] + ex[2] + ex[3]
    gates = [e / den for e in ex]
    hot = [lane_f == ik for ik in top_i]
    sel = jnp.zeros((tm, LANES), F32)
    wdense = jnp.zeros((tm, LANES), F32)
    for k in range(TOP_K):
        sel = sel + hot[k].astype(F32)
        wdense = wdense + jnp.where(hot[k], gates[k], 0.0)
    rows = lax.broadcasted_iota(jnp.int32, (tm, tm), 0)
    cols = lax.broadcasted_iota(jnp.int32, (tm, tm), 1)
    before = (cols < rows).astype(BF16)
    cnt = _dot(before, sel.astype(BF16)) + carry_scr[...]
    idx_o = jnp.zeros((tm, LANES), F32)
    gate_o = jnp.zeros((tm, LANES), F32)
    rank_o = jnp.zeros((tm, LANES), F32)
    for k in range(TOP_K):
        rk = jnp.sum(jnp.where(hot[k], cnt, 0.0), axis=1, keepdims=True)
        idx_o = jnp.where(lane == k, top_i[k], idx_o)
        gate_o = jnp.where(lane == k, gates[k], gate_o)
        rank_o = jnp.where(lane == k, rk, rank_o)
    idx_ref[...] = idx_o.astype(jnp.int32)
    gate_ref[...] = gate_o
    rank_ref[...] = rank_o.astype(jnp.int32)
    wd_ref[...] = wdense
    total = carry_scr[...] + jnp.sum(sel, axis=0, keepdims=True)
    carry_scr[...] = total
    cnt_ref[...] = total


def _mixout(of, os_, x, gt, sc, sh, gpm, gpf, wo, wr, br, tm):
    p, d = x.shape
    half = of.shape[1]
    row = lambda i: (i, 0)
    const = lambda i: (0, 0)
    mrow = row if gt.shape[0] == p else const
    msh = (tm, d) if gt.shape[0] == p else (1, d)
    return pl.pallas_call(
        _mixout_kernel,
        grid=(p // tm,),
        in_specs=[pl.BlockSpec((tm, half), row), pl.BlockSpec((tm, half), row), pl.BlockSpec((tm, d), row),
                  pl.BlockSpec(msh, mrow), pl.BlockSpec(msh, mrow), pl.BlockSpec(msh, mrow),
                  pl.BlockSpec((1, d), const), pl.BlockSpec((1, d), const),
                  pl.BlockSpec((2 * half, d), const),
                  pl.BlockSpec((d, LANES), const), pl.BlockSpec((1, LANES), const)],
        out_specs=[pl.BlockSpec((tm, d), row), pl.BlockSpec((tm, d), row),
                   pl.BlockSpec((tm, LANES), row), pl.BlockSpec((tm, LANES), row),
                   pl.BlockSpec((tm, LANES), row), pl.BlockSpec((tm, LANES), row),
                   pl.BlockSpec((1, LANES), const)],
        out_shape=[jax.ShapeDtypeStruct((p, d), F32), jax.ShapeDtypeStruct((p, d), F32),
                   jax.ShapeDtypeStruct((p, LANES), jnp.int32), jax.ShapeDtypeStruct((p, LANES), F32),
                   jax.ShapeDtypeStruct((p, LANES), jnp.int32), jax.ShapeDtypeStruct((p, LANES), F32),
                   jax.ShapeDtypeStruct((1, LANES), F32)],
        scratch_shapes=[pltpu.VMEM((1, LANES), F32)],
        compiler_params=_cp(("arbitrary",)),
        name="mixout_router",
    )(of, os_, x, gt, sc, sh, gpm, gpf, wo, wr, br)


def _dispatch_kernel(dest_ref, h_ref, xs_in, xs_out, sem, *, tt):
    del xs_in

    def copy(t, k):
        d = dest_ref[t * TOP_K + k]
        return pltpu.make_async_copy(h_ref.at[pl.ds(t, 1)], xs_out.at[pl.ds(d, 1)], sem)

    def start(t, c):
        for k in range(TOP_K):
            copy(t, k).start()
        return c

    def wait(t, c):
        for k in range(TOP_K):
            copy(t, k).wait()
        return c

    lax.fori_loop(0, tt, start, 0)
    lax.fori_loop(0, tt, wait, 0)


def _dispatch(dest_flat, h2, xs_zero, tt=128):
    p, d = h2.shape
    return pl.pallas_call(
        functools.partial(_dispatch_kernel, tt=tt),
        grid=(p // tt,),
        in_specs=[pl.BlockSpec((tt * TOP_K,), lambda i: (i,), memory_space=pltpu.SMEM),
                  pl.BlockSpec((tt, d), lambda i: (i, 0)),
                  pl.BlockSpec(memory_space=pl.ANY)],
        out_specs=pl.BlockSpec(memory_space=pl.ANY),
        out_shape=jax.ShapeDtypeStruct(xs_zero.shape, xs_zero.dtype),
        scratch_shapes=[pltpu.SemaphoreType.DMA(())],
        input_output_aliases={2: 0},
        compiler_params=_cp(("arbitrary",)),
        name="moe_dispatch",
    )(dest_flat, h2, xs_zero)


def _gate_up_kernel(te_ref, nu_ref, x_ref, wg_ref, wu_ref, bg_ref, bu_ref, h_ref, wgb, wub):
    i = pl.program_id(1)

    @pl.when(i < nu_ref[0])
    def _():
        prev = te_ref[jnp.maximum(i - 1, 0)]
        fresh = jnp.logical_or(i == 0, te_ref[i] != prev)

        @pl.when(fresh)
        def _():
            wgb[...] = wg_ref[...].astype(BF16)
            wub[...] = wu_ref[...].astype(BF16)

        x = x_ref[...].astype(BF16)
        g = jnp.minimum(_dot(x, wgb[...]) + bg_ref[...], SWIGLU_LIMIT)
        u = jnp.clip(_dot(x, wub[...]) + bu_ref[...], -SWIGLU_LIMIT, SWIGLU_LIMIT)
        act = g * (1.0 / (1.0 + jnp.exp(-SWIGLU_ALPHA * g))) * (u + 1.0)
        h_ref[...] = act.astype(h_ref.dtype)

    @pl.when(i >= nu_ref[0])
    def _():
        h_ref[...] = jnp.zeros_like(h_ref)


def _gate_up(tile_e, n_used, xs, w_gu, b_gu, tm, tn=512):
    r, d = xs.shape
    f = w_gu.shape[2] // 2
    nb = f // tn

    def tile(n, i, te, nu):
        return jnp.minimum(i, nu[0] - 1)

    return pl.pallas_call(
        _gate_up_kernel,
        grid_spec=pltpu.PrefetchScalarGridSpec(
            num_scalar_prefetch=2,
            grid=(nb, r // tm),
            in_specs=[pl.BlockSpec((tm, d), lambda n, i, te, nu: (tile(n, i, te, nu), 0)),
                      pl.BlockSpec((None, d, tn), lambda n, i, te, nu: (te[tile(n, i, te, nu)], 0, n)),
                      pl.BlockSpec((None, d, tn), lambda n, i, te, nu: (te[tile(n, i, te, nu)], 0, nb + n)),
                      pl.BlockSpec((None, 1, tn), lambda n, i, te, nu: (te[tile(n, i, te, nu)], 0, n)),
                      pl.BlockSpec((None, 1, tn), lambda n, i, te, nu: (te[tile(n, i, te, nu)], 0, nb + n))],
            out_specs=pl.BlockSpec((tm, tn), lambda n, i, te, nu: (i, n)),
            scratch_shapes=[pltpu.VMEM((d, tn), BF16), pltpu.VMEM((d, tn), BF16)]),
        out_shape=jax.ShapeDtypeStruct((r, f), BF16),
        compiler_params=_cp(("arbitrary", "arbitrary")),
        name="moe_gate_up",
    )(tile_e, n_used, xs, w_gu, w_gu, b_gu, b_gu)


def _down_kernel(te_ref, nu_ref, h_ref, w_ref, b_ref, y_ref, wb):
    i = pl.program_id(1)

    @pl.when(i < nu_ref[0])
    def _():
        prev = te_ref[jnp.maximum(i - 1, 0)]
        fresh = jnp.logical_or(i == 0, te_ref[i] != prev)

        @pl.when(fresh)
        def _():
            wb[...] = w_ref[...].astype(BF16)

        y_ref[...] = _dot(h_ref[...], wb[...]) + b_ref[...]

    @pl.when(i >= nu_ref[0])
    def _():
        y_ref[...] = jnp.zeros_like(y_ref)


def _down(tile_e, n_used, hs, w_dn, b_dn, tm, tn=1024):
    r, f = hs.shape
    d = w_dn.shape[2]

    def tile(n, i, te, nu):
        return jnp.minimum(i, nu[0] - 1)

    return pl.pallas_call(
        _down_kernel,
        grid_spec=pltpu.PrefetchScalarGridSpec(
            num_scalar_prefetch=2,
            grid=(d // tn, r // tm),
            in_specs=[pl.BlockSpec((tm, f), lambda n, i, te, nu: (tile(n, i, te, nu), 0)),
                      pl.BlockSpec((None, f, tn), lambda n, i, te, nu: (te[tile(n, i, te, nu)], 0, n)),
                      pl.BlockSpec((None, 1, tn), lambda n, i, te, nu: (te[tile(n, i, te, nu)], 0, n))],
            out_specs=pl.BlockSpec((tm, tn), lambda n, i, te, nu: (i, n)),
            scratch_shapes=[pltpu.VMEM((f, tn), BF16)]),
        out_shape=jax.ShapeDtypeStruct((r, d), F32),
        compiler_params=_cp(("arbitrary", "arbitrary")),
        name="moe_down",
    )(tile_e, n_used, hs, w_dn, b_dn)


def _combine_kernel(dest_ref, dnext_ref, y_hbm, gate_ref, x1_ref, gt_ref, g_ref, o_ref, buf, sems, *, tt):
    i = pl.program_id(0)
    slot = i % 2

    def copy(dref, s, t, k):
        d = dref[t * TOP_K + k]
        return pltpu.make_async_copy(y_hbm.at[pl.ds(d, 1)], buf.at[s, k, pl.ds(t, 1)], sems.at[s])

    def start_all(dref, s):
        def body(t, c):
            for k in range(TOP_K):
                copy(dref, s, t, k).start()
            return c
        lax.fori_loop(0, tt, body, 0)

    @pl.when(i == 0)
    def _():
        start_all(dest_ref, 0)

    @pl.when(i + 1 < pl.num_programs(0))
    def _():
        start_all(dnext_ref, 1 - slot)

    def wait(t, c):
        for k in range(TOP_K):
            copy(dest_ref, slot, t, k).wait()
        return c

    lax.fori_loop(0, tt, wait, 0)
    gate = gate_ref[...]
    ffn = gate[:, 0:1] * buf[slot, 0]
    for k in range(1, TOP_K):
        ffn = ffn + gate[:, k:k + 1] * buf[slot, k]
    o_ref[...] = x1_ref[...] + gt_ref[...] * _rms(ffn, g_ref[...])


def _combine(dest_flat, ys, gates, x1, gt, g, tt=128):
    p, d = x1.shape
    n = p // tt
    row = lambda i: (i, 0)
    const = lambda i: (0, 0)
    return pl.pallas_call(
        functools.partial(_combine_kernel, tt=tt),
        grid=(n,),
        in_specs=[pl.BlockSpec((tt * TOP_K,), lambda i: (i,), memory_space=pltpu.SMEM),
                  pl.BlockSpec((tt * TOP_K,), lambda i: (jnp.minimum(i + 1, n - 1),), memory_space=pltpu.SMEM),
                  pl.BlockSpec(memory_space=pl.ANY),
                  pl.BlockSpec((tt, LANES), row), pl.BlockSpec((tt, d), row),
                  pl.BlockSpec((1, d), const), pl.BlockSpec((1, d), const)],
        out_specs=pl.BlockSpec((tt, d), row),
        out_shape=jax.ShapeDtypeStruct((p, d), F32),
        scratch_shapes=[pltpu.VMEM((2, TOP_K, tt, d), F32), pltpu.SemaphoreType.DMA((2,))],
        compiler_params=_cp(("arbitrary",)),
        name="moe_combine",
    )(dest_flat, dest_flat, ys, gates, x1, gt, g)


def _moe_prompt(h2, top_idx, gates, rank, counts, x1, gt_m, g_post, w_gu, b_gu, w_dn, b_dn, tm=512):
    p, d = h2.shape
    n_tiles = (p * TOP_K) // tm + N_EXPERTS
    padded = (counts + tm - 1) // tm * tm
    pends = jnp.cumsum(padded)
    pstart = pends - padded
    dest = (pstart[top_idx] + rank).reshape(-1).astype(jnp.int32)
    bstart = jnp.arange(n_tiles, dtype=jnp.int32) * tm
    tile_e = jnp.minimum(jnp.sum(pends[None, :] <= bstart[:, None], axis=1), N_EXPERTS - 1).astype(jnp.int32)
    n_used = (pends[-1:] // tm).astype(jnp.int32)
    xs = _dispatch(dest, h2, jnp.zeros((n_tiles * tm, d), F32))
    hs = _gate_up(tile_e, n_used, xs, w_gu, b_gu[:, None, :], tm)
    ys = _down(tile_e, n_used, hs, w_dn, b_dn[:, None, :], tm)
    return _combine(dest, ys, gates, x1, gt_m, g_post)


def _inproj_sample_kernel(x_ref, sc_ref, sh_ref, g_ref, w_ref, bf_ref, z_ref, lf_ref, h_scr, *, nk, tk, off_f):
    k = pl.program_id(0)

    @pl.when(k == 0)
    def _():
        h = _rms(x_ref[...], g_ref[...]) * (1.0 + sc_ref[...]) + sh_ref[...]
        for kk in range(nk):
            h_scr[kk] = h[:, kk * tk:(kk + 1) * tk]
        z_ref[...] = jnp.zeros_like(z_ref)

    z_ref[...] += _dot(h_scr[k].astype(BF16), w_ref[...].astype(BF16))

    @pl.when(k == nk - 1)
    def _():
        lf_ref[...] = _log_sigmoid(z_ref[:, off_f:off_f + LANES] + bf_ref[...])


def _inproj_sample(x, sc, sh, g, w, bfor, off_f, tk=256):
    n, d = x.shape
    dn = w.shape[1]
    nk = d // tk
    const = lambda k: (0, 0)
    return pl.pallas_call(
        functools.partial(_inproj_sample_kernel, nk=nk, tk=tk, off_f=off_f),
        grid=(nk,),
        in_specs=[pl.BlockSpec((n, d), const), pl.BlockSpec((n, d), const), pl.BlockSpec((n, d), const),
                  pl.BlockSpec((1, d), const), pl.BlockSpec((tk, dn), lambda k: (k, 0)),
                  pl.BlockSpec((1, LANES), const)],
        out_specs=[pl.BlockSpec((n, dn), const), pl.BlockSpec((n, LANES), const)],
        out_shape=[jax.ShapeDtypeStruct((n, dn), F32), jax.ShapeDtypeStruct((n, LANES), F32)],
        scratch_shapes=[pltpu.VMEM((nk, n, tk), F32)],
        compiler_params=_cp(("arbitrary",)),
        name="inproj_sample",
    )(x, sc, sh, g, w, bfor)


def _decode_kernel(pt_ref, fq_ref, fkn_ref, fvn_ref, lfn_ref, sq_ref, bsb_ref, gf_ref, gs_ref,
                   fk_hbm, fv_hbm, lf_hbm, sk_hbm, sv_hbm, of_ref, os_ref,
                   buf, lfbuf, sems, lsem, m_scr, l_scr, acc_scr, dsuf_scr, sacc_scr, skeep_scr, *, n_pages):
    n = pl.program_id(0)
    grp = buf.shape[1]
    page = buf.shape[3]
    nh = fq_ref.shape[0]
    d = HEAD_DIM
    n_groups = n_pages // grp
    caches = (fk_hbm, fv_hbm, sk_hbm, sv_hbm)

    def fetch(seq, gi, slot):
        cps = []
        for g in range(grp):
            pg = pt_ref[seq, n_pages - 1 - gi * grp - g]
            cps += [pltpu.make_async_copy(hbm.at[pg, :, h, :], buf.at[slot, g, c, :, pl.ds(h * d, d)],
                                          sems.at[slot, c])
                    for c, hbm in enumerate(caches) for h in range(nh)]
            cps.append(pltpu.make_async_copy(lf_hbm.at[pg], lfbuf.at[slot, g], lsem.at[slot]))
        return cps

    first = n * n_groups

    @pl.when(n == 0)
    def _():
        for cp in fetch(0, 0, 0):
            cp.start()

    sub = lax.broadcasted_iota(jnp.int32, (nh, d), 0)
    sub_w = lax.broadcasted_iota(jnp.int32, (nh, nh * d), 0)
    blk_w = lax.broadcasted_iota(jnp.int32, (nh, nh * d), 1) // d

    def block_diag(q):
        return jnp.where(sub_w == blk_w, jnp.concatenate([q] * nh, axis=1), jnp.zeros((), q.dtype))

    def diag_blocks(o_full):
        o = jnp.zeros((nh, d), F32)
        for h in range(nh):
            o = jnp.where(sub == h, o_full[:, h * d:(h + 1) * d], o)
        return o

    q = fq_ref[...].astype(BF16)
    qf_bd = block_diag(q)
    qs_bd = block_diag(sq_ref[...].astype(BF16))
    kn = fkn_ref[...].astype(BF16).astype(F32)
    m_scr[...] = jnp.sum(q.astype(F32) * kn, axis=1, keepdims=True) * ATTN_SCALE
    l_scr[...] = jnp.ones_like(l_scr)
    acc_scr[...] = fvn_ref[...].astype(BF16).astype(F32)
    dsuf_scr[...] = jnp.zeros_like(dsuf_scr)
    sacc_scr[...] = jnp.zeros_like(sacc_scr)
    skeep_scr[...] = jnp.zeros_like(skeep_scr)

    rr = lax.broadcasted_iota(jnp.int32, (2 * page, page), 0) % page
    cc = lax.broadcasted_iota(jnp.int32, (2 * page, page), 1)
    later2 = (rr > cc).astype(BF16)

    def later_sums(x):
        hi = x.astype(BF16)
        lo = (x - hi.astype(F32)).astype(BF16)
        return _dot(jnp.concatenate([hi, lo], axis=1), later2)

    def body(gi, carry):
        slot = (first + gi) % 2

        @pl.when(gi + 1 < n_groups)
        def _():
            for cp in fetch(n, gi + 1, 1 - slot):
                cp.start()

        @pl.when(jnp.logical_and(gi + 1 == n_groups, n + 1 < pl.num_programs(0)))
        def _():
            for cp in fetch(n + 1, 0, 1 - slot):
                cp.start()

        for cp in fetch(n, gi, slot):
            cp.wait()

        dsuf = dsuf_scr[...]
        scs, pvs = [], None
        for g in range(grp):
            lf = lfbuf[slot, g]
            s = _dot_nt(qf_bd, buf[slot, g, 0].astype(BF16)) * ATTN_SCALE
            scs.append(s + lfn_ref[...] + later_sums(lf) + dsuf)
            dsuf = dsuf + jnp.sum(lf, axis=1, keepdims=True)
        dsuf_scr[...] = dsuf
        sc = jnp.concatenate(scs, axis=1)
        m_prev = m_scr[...]
        m_new = jnp.maximum(m_prev, jnp.max(sc, axis=1, keepdims=True))
        corr = jnp.exp(m_prev - m_new)
        pr = jnp.exp(sc - m_new)
        l_scr[...] = l_scr[...] * corr + jnp.sum(pr, axis=1, keepdims=True)
        for g in range(grp):
            o = _dot(pr[:, g * page:(g + 1) * page].astype(BF16), buf[slot, g, 1].astype(BF16))
            pvs = o if pvs is None else pvs + o
        acc_scr[...] = acc_scr[...] * corr + diag_blocks(pvs)
        m_scr[...] = m_new

        keep = skeep_scr[...]
        avs = None
        for g in range(grp):
            z = _dot_nt(qs_bd, buf[slot, g, 2].astype(BF16)) * ATTN_SCALE + bsb_ref[...]
            sp = _softplus(z)
            a = jnp.exp(z - sp - later_sums(sp) - keep)
            keep = keep + jnp.sum(sp, axis=1, keepdims=True)
            o = _dot(a.astype(BF16), buf[slot, g, 3].astype(BF16))
            avs = o if avs is None else avs + o
        skeep_scr[...] = keep
        sacc_scr[...] += diag_blocks(avs)
        return carry

    lax.fori_loop(0, n_groups, body, 0)
    of_ref[...] = _rms(acc_scr[...] / l_scr[...], gf_ref[...])
    os_ref[...] = _rms(sacc_scr[...], gs_ref[...])


def _decode(page_table, fq, fkn, fvn, lfn, sq, bsb, gf, gs, ck, cv, clf_t, csk, csv):
    n, n_pages = page_table.shape
    grp = math.gcd(n_pages, 8)
    nh, d = fq.shape[1], fq.shape[2]
    page = ck.shape[1]
    per_seq = lambda i, pt: (i, 0, 0)
    const = lambda i, pt: (0, 0)
    vec = pl.BlockSpec((None, nh, d), per_seq)
    col = pl.BlockSpec((None, nh, 1), per_seq)
    hbm = pl.BlockSpec(memory_space=pl.ANY)
    return pl.pallas_call(
        functools.partial(_decode_kernel, n_pages=n_pages),
        grid_spec=pltpu.PrefetchScalarGridSpec(
            num_scalar_prefetch=1,
            grid=(n,),
            in_specs=[vec, vec, vec, col, vec,
                      pl.BlockSpec((nh, 1), const), pl.BlockSpec((nh, d), const), pl.BlockSpec((nh, d), const),
                      hbm, hbm, hbm, hbm, hbm],
            out_specs=[vec, vec],
            scratch_shapes=[pltpu.VMEM((2, grp, 4, page, nh * d), F32), pltpu.VMEM((2, grp, nh, page), F32),
                            pltpu.SemaphoreType.DMA((2, 4)), pltpu.SemaphoreType.DMA((2,)),
                            pltpu.VMEM((nh, 1), F32), pltpu.VMEM((nh, 1), F32), pltpu.VMEM((nh, d), F32),
                            pltpu.VMEM((nh, 1), F32), pltpu.VMEM((nh, d), F32), pltpu.VMEM((nh, 1), F32)]),
        out_shape=[jax.ShapeDtypeStruct((n, nh, d), F32), jax.ShapeDtypeStruct((n, nh, d), F32)],
        compiler_params=_cp(("arbitrary",)),
        name="decode_attn",
    )(page_table, fq, fkn, fvn, lfn, sq, bsb, gf, gs, ck, cv, clf_t, csk, csv)


def _moe_small_kernel(h_ref, wd_ref, wg_ref, wu_ref, bg_ref, bu_ref, wdn_ref, bd_ref, x1_ref, gt_ref, g_ref,
                      y_ref, acc_scr, *, n_e, n_j):
    e = pl.program_id(0)
    j = pl.program_id(1)
    n = h_ref.shape[0]

    @pl.when(jnp.logical_and(e == 0, j == 0))
    def _():
        acc_scr[...] = jnp.zeros_like(acc_scr)

    lane = lax.broadcasted_iota(jnp.int32, (n, LANES), 1)
    we = jnp.sum(jnp.where(lane == e, wd_ref[...], 0.0), axis=1, keepdims=True)
    hb = h_ref[...].astype(BF16)
    g = jnp.minimum(_dot(hb, wg_ref[...].astype(BF16)) + bg_ref[...], SWIGLU_LIMIT)
    u = jnp.clip(_dot(hb, wu_ref[...].astype(BF16)) + bu_ref[...], -SWIGLU_LIMIT, SWIGLU_LIMIT)
    act = g * (1.0 / (1.0 + jnp.exp(-SWIGLU_ALPHA * g))) * (u + 1.0)
    y = _dot(act.astype(BF16), wdn_ref[...].astype(BF16))
    acc_scr[...] += we * y

    @pl.when(j == 0)
    def _():
        acc_scr[...] += we * bd_ref[...]

    @pl.when(jnp.logical_and(e == n_e - 1, j == n_j - 1))
    def _():
        y_ref[...] = x1_ref[...] + gt_ref[...] * _rms(acc_scr[...], g_ref[...])


def _moe_small(h2, wdense, w_gu, b_gu, w_dn, b_dn, x1, gt, g, tf=512):
    n, d = h2.shape
    n_e = w_gu.shape[0]
    f = w_gu.shape[2] // 2
    nj = f // tf
    const = lambda e, j: (0, 0)
    return pl.pallas_call(
        functools.partial(_moe_small_kernel, n_e=n_e, n_j=nj),
        grid=(n_e, nj),
        in_specs=[pl.BlockSpec((n, d), const), pl.BlockSpec((n, LANES), const),
                  pl.BlockSpec((None, d, tf), lambda e, j: (e, 0, j)),
                  pl.BlockSpec((None, d, tf), lambda e, j: (e, 0, nj + j)),
                  pl.BlockSpec((None, 1, tf), lambda e, j: (e, 0, j)),
                  pl.BlockSpec((None, 1, tf), lambda e, j: (e, 0, nj + j)),
                  pl.BlockSpec((None, tf, d), lambda e, j: (e, j, 0)),
                  pl.BlockSpec((None, 1, d), lambda e, j: (e, 0, 0)),
                  pl.BlockSpec((n, d), const), pl.BlockSpec((n, d), const), pl.BlockSpec((1, d), const)],
        out_specs=pl.BlockSpec((n, d), const),
        out_shape=jax.ShapeDtypeStruct((n, d), F32),
        scratch_shapes=[pltpu.VMEM((n, d), F32)],
        compiler_params=_cp(("arbitrary", "arbitrary")),
        name="moe_sample",
    )(h2, wdense, w_gu, w_gu, b_gu, b_gu, w_dn, b_dn, x1, gt, g)


def _pad_lanes(a, fill=0.0):
    return jnp.pad(a, ((0, 0), (0, LANES - a.shape[1])), constant_values=fill)


def kernel(x_prompt, x_sample, cache_fox_k, cache_fox_v, cache_fox_logf, cache_sb_k, cache_sb_v, page_table,
           c_prompt, c_sample, w_ada, b_ada, g_pre_mix, w_in, b_forget, b_sb, g_fox_out, g_sb_out, w_out,
           g_post_mix, g_pre_ffn, w_router, b_router, w_gate_up, b_gate_up, w_down, b_down, g_post_ffn):
    depth = w_ada.shape[0]
    assert depth == 1 and x_sample.shape[1] == 1 and x_prompt.shape[0] == 1
    bsz, seq, d = x_prompt.shape
    ns = x_sample.shape[0]
    p = bsz * seq
    dfox = N_FOX * HEAD_DIM
    dsb = N_SB * HEAD_DIM
    off_ff = 3 * dfox
    off_sq = off_ff + N_FOX
    xp = x_prompt.reshape(p, d)
    xs = x_sample.reshape(ns, d)

    n_c = bsz + ns
    c_all = jnp.concatenate([c_prompt, c_sample, jnp.zeros((-n_c % 8, d), F32)], axis=0)
    mod = _ada(c_all, w_ada[0], b_ada[0][None])
    sh_a, sc_a, gt_a, sh_m, sc_m, gt_m = [mod[0:bsz, k * d:(k + 1) * d] for k in range(6)]
    ssh_a, ssc_a, sgt_a, ssh_m, ssc_m, sgt_m = [mod[bsz:n_c, k * d:(k + 1) * d] for k in range(6)]

    w_in0 = w_in[0]
    bfor = _pad_lanes(b_forget[0][None])
    gpre = g_pre_mix[0][None]
    gf = g_fox_out[0]
    gs = g_sb_out[0]
    wr = _pad_lanes(w_router[0]).astype(BF16)
    wo = w_out[0].astype(BF16)
    br = _pad_lanes(b_router[0][None], NEG_BIG)
    gpm = g_post_mix[0][None]
    gpf = g_pre_ffn[0][None]
    gpo = g_post_ffn[0][None]

    w6 = jnp.concatenate([w_in0[:, :off_ff], w_in0[:, off_sq:]], axis=1).astype(BF16)
    wf = _pad_lanes(w_in0[:, off_ff:off_sq]).astype(BF16)
    z, zb, lf, fcum = _inproj_prompt(xp, sc_a, sh_a, gpre, w6, wf, bfor)
    tkf = 512
    nf2 = (-LOG2E * fcum[:, :N_FOX]).T.reshape(N_FOX, p // tkf, 1, tkf)
    hb = dfox // HEAD_DIM
    o_fox = _fox_attn(zb, nf2, gf[:, None, :], 0, hb, 2 * hb, tq=tkf)
    o_sb = _sb_attn(zb, b_sb[0], gs[:, None, :], 3 * hb, 4 * hb, 5 * hb)
    x1, h2, top_idx, gates, rank, _, counts = _mixout(
        o_fox, o_sb, xp, gt_a, sc_m, sh_m, gpm, gpf, wo, wr, br, tm=256)
    cnt = counts[0, :N_EXPERTS].astype(jnp.int32)
    y_p = _moe_prompt(h2, top_idx[:, :TOP_K], gates, rank[:, :TOP_K], cnt, x1, gt_m, gpo,
                      w_gate_up[0], b_gate_up[0], w_down[0], b_down[0])

    zs, lfs = _inproj_sample(xs, ssc_a, ssh_a, gpre, w_in0, bfor, off_ff)
    heads = lambda a: a.reshape(ns, -1, HEAD_DIM)
    fq_s, fk_s, fv_s = heads(zs[:, :dfox]), heads(zs[:, dfox:2 * dfox]), heads(zs[:, 2 * dfox:off_ff])
    sq_s = heads(zs[:, off_sq:off_sq + dsb])
    sk_s = heads(zs[:, off_sq + dsb:off_sq + 2 * dsb])
    sv_s = heads(zs[:, off_sq + 2 * dsb:])
    lf_s = lfs[:, :N_FOX]
    drop0 = lambda a: a.reshape(a.shape[1:])
    of_s, os_s = _decode(page_table, fq_s, fk_s, fv_s, lf_s[:, :, None], sq_s, b_sb[0][:, None], gf, gs,
                         drop0(cache_fox_k), drop0(cache_fox_v), jnp.swapaxes(drop0(cache_fox_logf), 1, 2),
                         drop0(cache_sb_k), drop0(cache_sb_v))
    x1s, h2s, _, _, _, wdense, _ = _mixout(
        of_s.reshape(ns, dfox), os_s.reshape(ns, dsb), xs, sgt_a, ssc_m, ssh_m, gpm, gpf, wo, wr, br, tm=ns)
    y_s = _moe_small(h2s, wdense, w_gate_up[0], b_gate_up[0][:, None, :], w_down[0], b_down[0][:, None, :],
                     x1s, sgt_m, gpo)

    rows5 = lambda a, nh: a.reshape(1, bsz, seq, nh, HEAD_DIM)
    srows5 = lambda a: a.reshape(1, ns, 1, -1, HEAD_DIM)
    return (y_p.reshape(bsz, seq, d), y_s.reshape(ns, 1, d),
            rows5(z[:, dfox:2 * dfox], N_FOX), rows5(z[:, 2 * dfox:3 * dfox], N_FOX),
            lf[:, :N_FOX].reshape(1, bsz, seq, N_FOX),
            rows5(z[:, 3 * dfox + dsb:3 * dfox + 2 * dsb], N_SB), rows5(z[:, 3 * dfox + 2 * dsb:], N_SB),
            srows5(fk_s), srows5(fv_s), lf_s.reshape(1, ns, 1, N_FOX), srows5(sk_s), srows5(sv_s))
```

```python
import functools
import math

import jax
import jax.numpy as jnp
from jax import lax
from jax.experimental import pallas as pl
from jax.experimental.pallas import tpu as pltpu

F32 = jnp.float32
BF16 = jnp.bfloat16
HI = lax.Precision.HIGHEST

HEAD_DIM = 128
N_FOX = 8
N_SB = 8
TOP_K = 4
N_EXPERTS = 32
LANES = 128
NORM_EPS = 1e-6
SWIGLU_ALPHA = 1.702
SWIGLU_LIMIT = 7.0
ATTN_SCALE = HEAD_DIM ** -0.5
LOG2E = math.log2(math.e)
VMEM_LIMIT = 56 * 1024 * 1024
NEG_BIG = -1e30


def _cp(sem, vmem=VMEM_LIMIT, flags=None):
    return pltpu.CompilerParams(dimension_semantics=sem, vmem_limit_bytes=vmem, flags=flags)


def _rms(x, g):
    return x * lax.rsqrt(jnp.mean(x * x, axis=-1, keepdims=True) + NORM_EPS) * g


def _log_sigmoid(x):
    return jnp.minimum(x, 0.0) - jnp.log(1.0 + jnp.exp(-jnp.abs(x)))


def _softplus(x):
    return jnp.maximum(x, 0.0) + jnp.log(1.0 + jnp.exp2(jnp.abs(x) * (-LOG2E)))


def _dot(a, b, precision=None):
    return jnp.dot(a, b, preferred_element_type=F32, precision=precision)


def _dot_nt(a, b):
    return lax.dot_general(a, b, (((1,), (1,)), ((), ())), preferred_element_type=F32)


def _ada_kernel(c_ref, w_ref, b_ref, o_ref):
    c = c_ref[...]
    s = c / (1.0 + jnp.exp(-c))
    o_ref[...] = _dot(s.astype(BF16), w_ref[...].astype(BF16)) + b_ref[...]


def _ada(c_all, w, b, tn=1024):
    r, d = c_all.shape
    n = w.shape[1]
    return pl.pallas_call(
        _ada_kernel,
        grid=(n // tn,),
        in_specs=[pl.BlockSpec((r, d), lambda j: (0, 0)),
                  pl.BlockSpec((d, tn), lambda j: (0, j)),
                  pl.BlockSpec((1, tn), lambda j: (0, j))],
        out_specs=pl.BlockSpec((r, tn), lambda j: (0, j)),
        out_shape=jax.ShapeDtypeStruct((r, n), F32),
        compiler_params=_cp(("arbitrary",)),
        name="ada_mod",
    )(c_all, w, b)


def _inproj_kernel(x_ref, sc_ref, sh_ref, g_ref, w_ref, wf_ref, bf_ref,
                   z_ref, zb_ref, lf_ref, fc_ref, h_scr, carry_scr):
    i = pl.program_id(0)
    j = pl.program_id(1)
    tm = x_ref.shape[0]

    @pl.when(j == 0)
    def _():
        h = _rms(x_ref[...], g_ref[...]) * (1.0 + sc_ref[...]) + sh_ref[...]
        hb = h.astype(BF16)
        h_scr[...] = hb
        lf = _log_sigmoid(_dot(hb, wf_ref[...]) + bf_ref[...])
        lf_ref[...] = lf

        @pl.when(i == 0)
        def _():
            carry_scr[...] = jnp.zeros_like(carry_scr)

        rows = lax.broadcasted_iota(jnp.int32, (tm, tm), 0)
        cols = lax.broadcasted_iota(jnp.int32, (tm, tm), 1)
        tri = (cols <= rows).astype(F32)
        csum = _dot(tri, lf, HI) + carry_scr[...]
        fc_ref[...] = csum
        carry_scr[...] = csum[tm - 1:tm, :]

    z = _dot(h_scr[...], w_ref[...])
    z_ref[...] = z
    zb_ref[...] = z.astype(BF16)


def _inproj_prompt(x, sc, sh, g, w6, wf, bfor, tm=512, tn=1024):
    p, d = x.shape
    n = w6.shape[1]
    row = lambda i, j: (i, 0)
    const = lambda i, j: (0, 0)
    return pl.pallas_call(
        _inproj_kernel,
        grid=(p // tm, n // tn),
        in_specs=[pl.BlockSpec((tm, d), row),
                  pl.BlockSpec((1, d), const), pl.BlockSpec((1, d), const), pl.BlockSpec((1, d), const),
                  pl.BlockSpec((d, tn), lambda i, j: (0, j)),
                  pl.BlockSpec((d, LANES), const), pl.BlockSpec((1, LANES), const)],
        out_specs=[pl.BlockSpec((tm, tn), lambda i, j: (i, j)),
                   pl.BlockSpec((tm, tn), lambda i, j: (i, j)),
                   pl.BlockSpec((tm, LANES), row),
                   pl.BlockSpec((tm, LANES), row)],
        out_shape=[jax.ShapeDtypeStruct((p, n), F32), jax.ShapeDtypeStruct((p, n), BF16),
                   jax.ShapeDtypeStruct((p, LANES), F32), jax.ShapeDtypeStruct((p, LANES), F32)],
        scratch_shapes=[pltpu.VMEM((tm, d), BF16), pltpu.VMEM((1, LANES), F32)],
        compiler_params=_cp(("arbitrary", "arbitrary")),
        name="inproj_prompt",
    )(x, sc, sh, g, w6, wf, bfor)


CHUNK_UNROLLS = (8, 2, 1)


def _unrolled_loops(n, step):
    done = 0
    for u in CHUNK_UNROLLS:
        trips = (n - done) // u

        def body(g, c, u=u, done=done):
            for k in range(u):
                step(done + g * u + k)
            return c

        lax.fori_loop(0, trips, body, 0)
        done = done + trips * u


def _fox_kernel(q_ref, k_ref, v_ref, nf_ref, g_ref, o_ref, m_scr, acc_scr, *, tq):
    i = pl.program_id(1)
    d = HEAD_DIM
    tk = tq
    q = (q_ref[...].astype(F32) * (ATTN_SCALE * LOG2E)).astype(BF16)
    m_scr[...] = jnp.full_like(m_scr, -jnp.inf)
    acc_scr[...] = jnp.zeros_like(acc_scr)
    ones = jnp.ones((tk, d), BF16)

    def update(j, masked):
        k = k_ref[pl.ds(pl.multiple_of(j * tk, tk), tk), :]
        v = v_ref[pl.ds(pl.multiple_of(j * tk, tk), tk), :]
        s = _dot_nt(q, k) + nf_ref[j]
        if masked:
            rows = lax.broadcasted_iota(jnp.int32, (tq, tk), 0)
            cols = lax.broadcasted_iota(jnp.int32, (tq, tk), 1)
            s = jnp.where(cols <= rows, s, -jnp.inf)
        m_prev = m_scr[...]
        m_new = jnp.maximum(m_prev, jnp.max(s, axis=1, keepdims=True))
        alpha = jnp.exp2(m_prev - m_new)
        p = jnp.exp2(s - jnp.tile(m_new, (1, tk // LANES)))
        pv = _dot(p.astype(BF16), jnp.concatenate([v, ones], axis=1))
        acc_scr[...] = jnp.tile(alpha, (1, 2 * d // LANES)) * acc_scr[...] + pv
        m_scr[...] = m_new

    _unrolled_loops(i, lambda j: update(j, False))
    update(i, True)
    o = acc_scr[:, :d] / acc_scr[:, d:]
    o_ref[...] = _rms(o, g_ref[...]).astype(o_ref.dtype)


def _fox_attn(zb, nf2, g, col_q, col_k, col_v, tq):
    p = zb.shape[0]
    d = HEAD_DIM
    tk = tq
    return pl.pallas_call(
        functools.partial(_fox_kernel, tq=tq),
        grid=(N_FOX, p // tq),
        in_specs=[pl.BlockSpec((tq, d), lambda h, i: (i, col_q + h)),
                  pl.BlockSpec((p, d), lambda h, i: (0, col_k + h)),
                  pl.BlockSpec((p, d), lambda h, i: (0, col_v + h)),
                  pl.BlockSpec((None, p // tk, 1, tk), lambda h, i: (h, 0, 0, 0)),
                  pl.BlockSpec((None, 1, d), lambda h, i: (h, 0, 0))],
        out_specs=pl.BlockSpec((tq, d), lambda h, i: (i, h)),
        out_shape=jax.ShapeDtypeStruct((p, N_FOX * d), BF16),
        scratch_shapes=[pltpu.VMEM((tq, LANES), F32), pltpu.VMEM((tq, 2 * d), F32)],
        compiler_params=_cp(("arbitrary", "arbitrary")),
        name="fox_attn",
    )(zb, zb, zb, nf2, g)


SB_CHUNK = 2 * LANES


def _sb_kernel(q_ref, k_ref, v_ref, b_ref, g_ref, tri_ref, o_ref, carry_scr, acc_scr, *, tq):
    i = pl.program_id(1)
    tk = SB_CHUNK
    d = HEAD_DIM
    lane = lax.broadcasted_iota(jnp.int32, (tq, d), 1)
    q = jnp.concatenate([(q_ref[...].astype(F32) * ATTN_SCALE).astype(BF16),
                         jnp.where(lane < 2, 1.0, 0.0).astype(BF16)], axis=1)
    carry_scr[...] = jnp.zeros_like(carry_scr)
    acc_scr[...] = jnp.zeros_like(acc_scr)
    zeros_v = jnp.zeros((tk, d), BF16)

    def step(j, masked):
        k = k_ref[pl.ds(pl.multiple_of(j * tk, tk), tk), :]
        v = v_ref[pl.ds(pl.multiple_of(j * tk, tk), tk), :]
        z = _dot_nt(q, jnp.concatenate([k, b_ref[...]], axis=1))
        sp = _softplus(z)
        if masked:
            rows = i * tq + lax.broadcasted_iota(jnp.int32, (tq, tk), 0)
            cols = j * tk + lax.broadcasted_iota(jnp.int32, (tq, tk), 1)
            valid = cols < rows
            sp = jnp.where(valid, sp, 0.0)
        carry = carry_scr[...]
        r = _dot(sp.astype(BF16), tri_ref[...])
        a = jnp.exp(z - r - jnp.tile(carry, (1, tk // LANES)))
        if masked:
            a = jnp.where(valid, a, 0.0)
        carry_scr[...] = carry + r[:, 0:1]
        acc_scr[...] += _dot(a.astype(BF16), jnp.concatenate([v, zeros_v], axis=1))

    n_full = (i * tq) // tk
    for dgn in reversed(range(tq // tk)):
        step(n_full + dgn, True)

    _unrolled_loops(n_full, lambda jj: step(n_full - 1 - jj, False))
    o_ref[...] = _rms(acc_scr[:, :d], g_ref[...]).astype(o_ref.dtype)


def _sb_attn(zb, b_sb, g, col_q, col_k, col_v, tq=512):
    p = zb.shape[0]
    d = HEAD_DIM
    r = lax.broadcasted_iota(jnp.int32, (SB_CHUNK, SB_CHUNK), 0)
    c = lax.broadcasted_iota(jnp.int32, (SB_CHUNK, SB_CHUNK), 1)
    tri = (r >= c).astype(BF16)
    b_hi = b_sb.astype(BF16)
    b_lo = (b_sb - b_hi.astype(F32)).astype(BF16)
    lane = lax.broadcasted_iota(jnp.int32, (N_SB, SB_CHUNK, d), 2)
    bias = jnp.where(lane == 0, b_hi[:, None, None], jnp.where(lane == 1, b_lo[:, None, None], 0)).astype(BF16)
    return pl.pallas_call(
        functools.partial(_sb_kernel, tq=tq),
        grid=(N_SB, p // tq),
        in_specs=[pl.BlockSpec((tq, d), lambda h, i: (i, col_q + h)),
                  pl.BlockSpec((p, d), lambda h, i: (0, col_k + h)),
                  pl.BlockSpec((p, d), lambda h, i: (0, col_v + h)),
                  pl.BlockSpec((None, SB_CHUNK, d), lambda h, i: (h, 0, 0)),
                  pl.BlockSpec((None, 1, d), lambda h, i: (h, 0, 0)),
                  pl.BlockSpec((SB_CHUNK, SB_CHUNK), lambda h, i: (0, 0))],
        out_specs=pl.BlockSpec((tq, d), lambda h, i: (i, h)),
        out_shape=jax.ShapeDtypeStruct((p, N_SB * d), BF16),
        scratch_shapes=[pltpu.VMEM((tq, LANES), F32), pltpu.VMEM((tq, 2 * d), F32)],
        compiler_params=_cp(("arbitrary", "arbitrary")),
        name="sb_attn",
    )(zb, zb, zb, bias, g, tri)


def _mixout_kernel(of_ref, os_ref, x_ref, gt_ref, sc_ref, sh_ref, gpm_ref, gpf_ref, wo_ref, wr_ref, br_ref,
                   x1_ref, h2_ref, idx_ref, gate_ref, rank_ref, wd_ref, cnt_ref, carry_scr):
    i = pl.program_id(0)
    tm = x_ref.shape[0]
    half = of_ref.shape[1]

    @pl.when(i == 0)
    def _():
        carry_scr[...] = jnp.zeros_like(carry_scr)

    mix = (_dot(of_ref[...].astype(BF16), wo_ref[0:half, :])
           + _dot(os_ref[...].astype(BF16), wo_ref[half:2 * half, :]))
    x1 = x_ref[...] + gt_ref[...] * _rms(mix, gpm_ref[...])
    x1_ref[...] = x1
    h2 = _rms(x1, gpf_ref[...]) * (1.0 + sc_ref[...]) + sh_ref[...]
    h2_ref[...] = h2
    logits = _dot(h2.astype(BF16), wr_ref[...]) + br_ref[...]
    lane = lax.broadcasted_iota(jnp.int32, (tm, LANES), 1)
    lane_f = lane.astype(F32)
    vals = logits
    top_i, top_v = [], []
    for _ in range(TOP_K):
        mk = jnp.max(vals, axis=1, keepdims=True)
        ik = jnp.min(jnp.where(vals == mk, lane_f, float(LANES)), axis=1, keepdims=True)
        vals = jnp.where(lane_f == ik, -jnp.inf, vals)
        top_i.append(ik)
        top_v.append(mk)
    ex = [jnp.exp(v - top_v[0]) for v in top_v]
    den = ex[0] + ex[1] + ex[2] + ex[3]
    gates = [e / den for e in ex]
    hot = [lane_f == ik for ik in top_i]
    sel = jnp.zeros((tm, LANES), F32)
    wdense = jnp.zeros((tm, LANES), F32)
    for k in range(TOP_K):
        sel = sel + hot[k].astype(F32)
        wdense = wdense + jnp.where(hot[k], gates[k], 0.0)
    rows = lax.broadcasted_iota(jnp.int32, (tm, tm), 0)
    cols = lax.broadcasted_iota(jnp.int32, (tm, tm), 1)
    before = (cols < rows).astype(BF16)
    cnt = _dot(before, sel.astype(BF16)) + carry_scr[...]
    idx_o = jnp.zeros((tm, LANES), F32)
    gate_o = jnp.zeros((tm, LANES), F32)
    rank_o = jnp.zeros((tm, LANES), F32)
    for k in range(TOP_K):
        rk = jnp.sum(jnp.where(hot[k], cnt, 0.0), axis=1, keepdims=True)
        idx_o = jnp.where(lane == k, top_i[k], idx_o)
        gate_o = jnp.where(lane == k, gates[k], gate_o)
        rank_o = jnp.where(lane == k, rk, rank_o)
    idx_ref[...] = idx_o.astype(jnp.int32)
    gate_ref[...] = gate_o
    rank_ref[...] = rank_o.astype(jnp.int32)
    wd_ref[...] = wdense
    total = carry_scr[...] + jnp.sum(sel, axis=0, keepdims=True)
    carry_scr[...] = total
    cnt_ref[...] = total


def _mixout(of, os_, x, gt, sc, sh, gpm, gpf, wo, wr, br, tm):
    p, d = x.shape
    half = of.shape[1]
    row = lambda i: (i, 0)
    const = lambda i: (0, 0)
    mrow = row if gt.shape[0] == p else const
    msh = (tm, d) if gt.shape[0] == p else (1, d)
    return pl.pallas_call(
        _mixout_kernel,
        grid=(p // tm,),
        in_specs=[pl.BlockSpec((tm, half), row), pl.BlockSpec((tm, half), row), pl.BlockSpec((tm, d), row),
                  pl.BlockSpec(msh, mrow), pl.BlockSpec(msh, mrow), pl.BlockSpec(msh, mrow),
                  pl.BlockSpec((1, d), const), pl.BlockSpec((1, d), const),
                  pl.BlockSpec((2 * half, d), const),
                  pl.BlockSpec((d, LANES), const), pl.BlockSpec((1, LANES), const)],
        out_specs=[pl.BlockSpec((tm, d), row), pl.BlockSpec((tm, d), row),
                   pl.BlockSpec((tm, LANES), row), pl.BlockSpec((tm, LANES), row),
                   pl.BlockSpec((tm, LANES), row), pl.BlockSpec((tm, LANES), row),
                   pl.BlockSpec((1, LANES), const)],
        out_shape=[jax.ShapeDtypeStruct((p, d), F32), jax.ShapeDtypeStruct((p, d), F32),
                   jax.ShapeDtypeStruct((p, LANES), jnp.int32), jax.ShapeDtypeStruct((p, LANES), F32),
                   jax.ShapeDtypeStruct((p, LANES), jnp.int32), jax.ShapeDtypeStruct((p, LANES), F32),
                   jax.ShapeDtypeStruct((1, LANES), F32)],
        scratch_shapes=[pltpu.VMEM((1, LANES), F32)],
        compiler_params=_cp(("arbitrary",)),
        name="mixout_router",
    )(of, os_, x, gt, sc, sh, gpm, gpf, wo, wr, br)


def _dispatch_kernel(dest_ref, h_ref, xs_in, xs_out, sem, *, tt):
    del xs_in

    def copy(t, k):
        d = dest_ref[t * TOP_K + k]
        return pltpu.make_async_copy(h_ref.at[pl.ds(t, 1)], xs_out.at[pl.ds(d, 1)], sem)

    def start(t, c):
        for k in range(TOP_K):
            copy(t, k).start()
        return c

    def wait(t, c):
        for k in range(TOP_K):
            copy(t, k).wait()
        return c

    lax.fori_loop(0, tt, start, 0)
    lax.fori_loop(0, tt, wait, 0)


def _dispatch(dest_flat, h2, xs_zero, tt=128):
    p, d = h2.shape
    return pl.pallas_call(
        functools.partial(_dispatch_kernel, tt=tt),
        grid=(p // tt,),
        in_specs=[pl.BlockSpec((tt * TOP_K,), lambda i: (i,), memory_space=pltpu.SMEM),
                  pl.BlockSpec((tt, d), lambda i: (i, 0)),
                  pl.BlockSpec(memory_space=pl.ANY)],
        out_specs=pl.BlockSpec(memory_space=pl.ANY),
        out_shape=jax.ShapeDtypeStruct(xs_zero.shape, xs_zero.dtype),
        scratch_shapes=[pltpu.SemaphoreType.DMA(())],
        input_output_aliases={2: 0},
        compiler_params=_cp(("arbitrary",)),
        name="moe_dispatch",
    )(dest_flat, h2, xs_zero)


def _gate_up_kernel(te_ref, nu_ref, x_ref, wg_ref, wu_ref, bg_ref, bu_ref, h_ref, wgb, wub):
    i = pl.program_id(1)

    @pl.when(i < nu_ref[0])
    def _():
        prev = te_ref[jnp.maximum(i - 1, 0)]
        fresh = jnp.logical_or(i == 0, te_ref[i] != prev)

        @pl.when(fresh)
        def _():
            wgb[...] = wg_ref[...].astype(BF16)
            wub[...] = wu_ref[...].astype(BF16)

        x = x_ref[...].astype(BF16)
        g = jnp.minimum(_dot(x, wgb[...]) + bg_ref[...], SWIGLU_LIMIT)
        u = jnp.clip(_dot(x, wub[...]) + bu_ref[...], -SWIGLU_LIMIT, SWIGLU_LIMIT)
        act = g * (1.0 / (1.0 + jnp.exp(-SWIGLU_ALPHA * g))) * (u + 1.0)
        h_ref[...] = act.astype(h_ref.dtype)

    @pl.when(i >= nu_ref[0])
    def _():
        h_ref[...] = jnp.zeros_like(h_ref)


def _gate_up(tile_e, n_used, xs, w_gu, b_gu, tm, tn=1024):
    r, d = xs.shape
    f = w_gu.shape[2] // 2
    nb = f // tn

    def tile(n, i, te, nu):
        return jnp.minimum(i, nu[0] - 1)

    return pl.pallas_call(
        _gate_up_kernel,
        grid_spec=pltpu.PrefetchScalarGridSpec(
            num_scalar_prefetch=2,
            grid=(nb, r // tm),
            in_specs=[pl.BlockSpec((tm, d), lambda n, i, te, nu: (tile(n, i, te, nu), 0)),
                      pl.BlockSpec((None, d, tn), lambda n, i, te, nu: (te[tile(n, i, te, nu)], 0, n)),
                      pl.BlockSpec((None, d, tn), lambda n, i, te, nu: (te[tile(n, i, te, nu)], 0, nb + n)),
                      pl.BlockSpec((None, 1, tn), lambda n, i, te, nu: (te[tile(n, i, te, nu)], 0, n)),
                      pl.BlockSpec((None, 1, tn), lambda n, i, te, nu: (te[tile(n, i, te, nu)], 0, nb + n))],
            out_specs=pl.BlockSpec((tm, tn), lambda n, i, te, nu: (i, n)),
            scratch_shapes=[pltpu.VMEM((d, tn), BF16), pltpu.VMEM((d, tn), BF16)]),
        out_shape=jax.ShapeDtypeStruct((r, f), BF16),
        compiler_params=_cp(("arbitrary", "arbitrary")),
        name="moe_gate_up",
    )(tile_e, n_used, xs, w_gu, w_gu, b_gu, b_gu)


def _down_kernel(te_ref, nu_ref, h_ref, w_ref, b_ref, y_ref, wb):
    i = pl.program_id(1)

    @pl.when(i < nu_ref[0])
    def _():
        prev = te_ref[jnp.maximum(i - 1, 0)]
        fresh = jnp.logical_or(i == 0, te_ref[i] != prev)

        @pl.when(fresh)
        def _():
            wb[...] = w_ref[...].astype(BF16)

        y_ref[...] = _dot(h_ref[...], wb[...]) + b_ref[...]

    @pl.when(i >= nu_ref[0])
    def _():
        y_ref[...] = jnp.zeros_like(y_ref)


def _down(tile_e, n_used, hs, w_dn, b_dn, tm, tn=1024):
    r, f = hs.shape
    d = w_dn.shape[2]

    def tile(n, i, te, nu):
        return jnp.minimum(i, nu[0] - 1)

    return pl.pallas_call(
        _down_kernel,
        grid_spec=pltpu.PrefetchScalarGridSpec(
            num_scalar_prefetch=2,
            grid=(d // tn, r // tm),
            in_specs=[pl.BlockSpec((tm, f), lambda n, i, te, nu: (tile(n, i, te, nu), 0)),
                      pl.BlockSpec((None, f, tn), lambda n, i, te, nu: (te[tile(n, i, te, nu)], 0, n)),
                      pl.BlockSpec((None, 1, tn), lambda n, i, te, nu: (te[tile(n, i, te, nu)], 0, n))],
            out_specs=pl.BlockSpec((tm, tn), lambda n, i, te, nu: (i, n)),
            scratch_shapes=[pltpu.VMEM((f, tn), BF16)]),
        out_shape=jax.ShapeDtypeStruct((r, d), F32),
        compiler_params=_cp(("arbitrary", "arbitrary")),
        name="moe_down",
    )(tile_e, n_used, hs, w_dn, b_dn)


def _combine_kernel(dest_ref, y_hbm, gate_ref, x1_ref, gt_ref, g_ref, o_ref, buf, sem, *, tt):
    def copy(t, k):
        d = dest_ref[t * TOP_K + k]
        return pltpu.make_async_copy(y_hbm.at[pl.ds(d, 1)], buf.at[k, pl.ds(t, 1)], sem)

    def start(t, c):
        for k in range(TOP_K):
            copy(t, k).start()
        return c

    def wait(t, c):
        for k in range(TOP_K):
            copy(t, k).wait()
        return c

    lax.fori_loop(0, tt, start, 0)
    lax.fori_loop(0, tt, wait, 0)
    gate = gate_ref[...]
    ffn = gate[:, 0:1] * buf[0]
    for k in range(1, TOP_K):
        ffn = ffn + gate[:, k:k + 1] * buf[k]
    o_ref[...] = x1_ref[...] + gt_ref[...] * _rms(ffn, g_ref[...])


def _combine(dest_flat, ys, gates, x1, gt, g, tt=128):
    p, d = x1.shape
    row = lambda i: (i, 0)
    const = lambda i: (0, 0)
    return pl.pallas_call(
        functools.partial(_combine_kernel, tt=tt),
        grid=(p // tt,),
        in_specs=[pl.BlockSpec((tt * TOP_K,), lambda i: (i,), memory_space=pltpu.SMEM),
                  pl.BlockSpec(memory_space=pl.ANY),
                  pl.BlockSpec((tt, LANES), row), pl.BlockSpec((tt, d), row),
                  pl.BlockSpec((1, d), const), pl.BlockSpec((1, d), const)],
        out_specs=pl.BlockSpec((tt, d), row),
        out_shape=jax.ShapeDtypeStruct((p, d), F32),
        scratch_shapes=[pltpu.VMEM((TOP_K, tt, d), F32), pltpu.SemaphoreType.DMA(())],
        compiler_params=_cp(("arbitrary",)),
        name="moe_combine",
    )(dest_flat, ys, gates, x1, gt, g)


def _moe_prompt(h2, top_idx, gates, rank, counts, x1, gt_m, g_post, w_gu, b_gu, w_dn, b_dn, tm=512):
    p, d = h2.shape
    n_tiles = (p * TOP_K) // tm + N_EXPERTS
    padded = (counts + tm - 1) // tm * tm
    pends = jnp.cumsum(padded)
    pstart = pends - padded
    dest = (pstart[top_idx] + rank).reshape(-1).astype(jnp.int32)
    bstart = jnp.arange(n_tiles, dtype=jnp.int32) * tm
    tile_e = jnp.minimum(jnp.sum(pends[None, :] <= bstart[:, None], axis=1), N_EXPERTS - 1).astype(jnp.int32)
    n_used = (pends[-1:] // tm).astype(jnp.int32)
    xs = _dispatch(dest, h2, jnp.zeros((n_tiles * tm, d), F32))
    hs = _gate_up(tile_e, n_used, xs, w_gu, b_gu[:, None, :], tm)
    ys = _down(tile_e, n_used, hs, w_dn, b_dn[:, None, :], tm)
    return _combine(dest, ys, gates, x1, gt_m, g_post)


def _inproj_sample_kernel(x_ref, sc_ref, sh_ref, g_ref, w_ref, bf_ref, z_ref, lf_ref, h_scr, *, nk, tk, off_f):
    k = pl.program_id(0)

    @pl.when(k == 0)
    def _():
        h = _rms(x_ref[...], g_ref[...]) * (1.0 + sc_ref[...]) + sh_ref[...]
        for kk in range(nk):
            h_scr[kk] = h[:, kk * tk:(kk + 1) * tk]
        z_ref[...] = jnp.zeros_like(z_ref)

    z_ref[...] += _dot(h_scr[k].astype(BF16), w_ref[...].astype(BF16))

    @pl.when(k == nk - 1)
    def _():
        lf_ref[...] = _log_sigmoid(z_ref[:, off_f:off_f + LANES] + bf_ref[...])


def _inproj_sample(x, sc, sh, g, w, bfor, off_f, tk=256):
    n, d = x.shape
    dn = w.shape[1]
    nk = d // tk
    const = lambda k: (0, 0)
    return pl.pallas_call(
        functools.partial(_inproj_sample_kernel, nk=nk, tk=tk, off_f=off_f),
        grid=(nk,),
        in_specs=[pl.BlockSpec((n, d), const), pl.BlockSpec((n, d), const), pl.BlockSpec((n, d), const),
                  pl.BlockSpec((1, d), const), pl.BlockSpec((tk, dn), lambda k: (k, 0)),
                  pl.BlockSpec((1, LANES), const)],
        out_specs=[pl.BlockSpec((n, dn), const), pl.BlockSpec((n, LANES), const)],
        out_shape=[jax.ShapeDtypeStruct((n, dn), F32), jax.ShapeDtypeStruct((n, LANES), F32)],
        scratch_shapes=[pltpu.VMEM((nk, n, tk), F32)],
        compiler_params=_cp(("arbitrary",)),
        name="inproj_sample",
    )(x, sc, sh, g, w, bfor)


def _decode_kernel(pt_ref, fq_ref, fkn_ref, fvn_ref, lfn_ref, sq_ref, bsb_ref, gf_ref, gs_ref,
                   fk_hbm, fv_hbm, lf_hbm, sk_hbm, sv_hbm, of_ref, os_ref,
                   buf, lfbuf, sems, lsem, m_scr, l_scr, acc_scr, dsuf_scr, sacc_scr, skeep_scr, *, n_pages):
    n = pl.program_id(0)
    grp = buf.shape[1]
    page = buf.shape[3]
    nh = fq_ref.shape[0]
    d = HEAD_DIM
    n_groups = n_pages // grp
    caches = (fk_hbm, fv_hbm, sk_hbm, sv_hbm)

    def fetch(seq, gi, slot):
        cps = []
        for g in range(grp):
            pg = pt_ref[seq, n_pages - 1 - gi * grp - g]
            cps += [pltpu.make_async_copy(hbm.at[pg, :, h, :], buf.at[slot, g, c, :, pl.ds(h * d, d)],
                                          sems.at[slot, c])
                    for c, hbm in enumerate(caches) for h in range(nh)]
            cps.append(pltpu.make_async_copy(lf_hbm.at[pg], lfbuf.at[slot, g], lsem.at[slot]))
        return cps

    first = n * n_groups

    @pl.when(n == 0)
    def _():
        for cp in fetch(0, 0, 0):
            cp.start()

    sub = lax.broadcasted_iota(jnp.int32, (nh, d), 0)
    sub_w = lax.broadcasted_iota(jnp.int32, (nh, nh * d), 0)
    blk_w = lax.broadcasted_iota(jnp.int32, (nh, nh * d), 1) // d

    def block_diag(q):
        return jnp.where(sub_w == blk_w, jnp.concatenate([q] * nh, axis=1), jnp.zeros((), q.dtype))

    def diag_blocks(o_full):
        o = jnp.zeros((nh, d), F32)
        for h in range(nh):
            o = jnp.where(sub == h, o_full[:, h * d:(h + 1) * d], o)
        return o

    q = fq_ref[...].astype(BF16)
    qf_bd = block_diag(q)
    qs_bd = block_diag(sq_ref[...].astype(BF16))
    kn = fkn_ref[...].astype(BF16).astype(F32)
    m_scr[...] = jnp.sum(q.astype(F32) * kn, axis=1, keepdims=True) * ATTN_SCALE
    l_scr[...] = jnp.ones_like(l_scr)
    acc_scr[...] = fvn_ref[...].astype(BF16).astype(F32)
    dsuf_scr[...] = jnp.zeros_like(dsuf_scr)
    sacc_scr[...] = jnp.zeros_like(sacc_scr)
    skeep_scr[...] = jnp.zeros_like(skeep_scr)

    rr = lax.broadcasted_iota(jnp.int32, (2 * page, page), 0) % page
    cc = lax.broadcasted_iota(jnp.int32, (2 * page, page), 1)
    later2 = (rr > cc).astype(BF16)

    def later_sums(x):
        hi = x.astype(BF16)
        lo = (x - hi.astype(F32)).astype(BF16)
        return _dot(jnp.concatenate([hi, lo], axis=1), later2)

    def body(gi, carry):
        slot = (first + gi) % 2

        @pl.when(gi + 1 < n_groups)
        def _():
            for cp in fetch(n, gi + 1, 1 - slot):
                cp.start()

        @pl.when(jnp.logical_and(gi + 1 == n_groups, n + 1 < pl.num_programs(0)))
        def _():
            for cp in fetch(n + 1, 0, 1 - slot):
                cp.start()

        for cp in fetch(n, gi, slot):
            cp.wait()

        dsuf = dsuf_scr[...]
        scs, pvs = [], None
        for g in range(grp):
            lf = lfbuf[slot, g]
            s = _dot_nt(qf_bd, buf[slot, g, 0].astype(BF16)) * ATTN_SCALE
            scs.append(s + lfn_ref[...] + later_sums(lf) + dsuf)
            dsuf = dsuf + jnp.sum(lf, axis=1, keepdims=True)
        dsuf_scr[...] = dsuf
        sc = jnp.concatenate(scs, axis=1)
        m_prev = m_scr[...]
        m_new = jnp.maximum(m_prev, jnp.max(sc, axis=1, keepdims=True))
        corr = jnp.exp(m_prev - m_new)
        pr = jnp.exp(sc - m_new)
        l_scr[...] = l_scr[...] * corr + jnp.sum(pr, axis=1, keepdims=True)
        for g in range(grp):
            o = _dot(pr[:, g * page:(g + 1) * page].astype(BF16), buf[slot, g, 1].astype(BF16))
            pvs = o if pvs is None else pvs + o
        acc_scr[...] = acc_scr[...] * corr + diag_blocks(pvs)
        m_scr[...] = m_new

        keep = skeep_scr[...]
        avs = None
        for g in range(grp):
            z = _dot_nt(qs_bd, buf[slot, g, 2].astype(BF16)) * ATTN_SCALE + bsb_ref[...]
            sp = _softplus(z)
            a = jnp.exp(z - sp - later_sums(sp) - keep)
            keep = keep + jnp.sum(sp, axis=1, keepdims=True)
            o = _dot(a.astype(BF16), buf[slot, g, 3].astype(BF16))
            avs = o if avs is None else avs + o
        skeep_scr[...] = keep
        sacc_scr[...] += diag_blocks(avs)
        return carry

    lax.fori_loop(0, n_groups, body, 0)
    of_ref[...] = _rms(acc_scr[...] / l_scr[...], gf_ref[...])
    os_ref[...] = _rms(sacc_scr[...], gs_ref[...])


def _decode(page_table, fq, fkn, fvn, lfn, sq, bsb, gf, gs, ck, cv, clf_t, csk, csv):
    n, n_pages = page_table.shape
    grp = math.gcd(n_pages, 8)
    nh, d = fq.shape[1], fq.shape[2]
    page = ck.shape[1]
    per_seq = lambda i, pt: (i, 0, 0)
    const = lambda i, pt: (0, 0)
    vec = pl.BlockSpec((None, nh, d), per_seq)
    col = pl.BlockSpec((None, nh, 1), per_seq)
    hbm = pl.BlockSpec(memory_space=pl.ANY)
    return pl.pallas_call(
        functools.partial(_decode_kernel, n_pages=n_pages),
        grid_spec=pltpu.PrefetchScalarGridSpec(
            num_scalar_prefetch=1,
            grid=(n,),
            in_specs=[vec, vec, vec, col, vec,
                      pl.BlockSpec((nh, 1), const), pl.BlockSpec((nh, d), const), pl.BlockSpec((nh, d), const),
                      hbm, hbm, hbm, hbm, hbm],
            out_specs=[vec, vec],
            scratch_shapes=[pltpu.VMEM((2, grp, 4, page, nh * d), F32), pltpu.VMEM((2, grp, nh, page), F32),
                            pltpu.SemaphoreType.DMA((2, 4)), pltpu.SemaphoreType.DMA((2,)),
                            pltpu.VMEM((nh, 1), F32), pltpu.VMEM((nh, 1), F32), pltpu.VMEM((nh, d), F32),
                            pltpu.VMEM((nh, 1), F32), pltpu.VMEM((nh, d), F32), pltpu.VMEM((nh, 1), F32)]),
        out_shape=[jax.ShapeDtypeStruct((n, nh, d), F32), jax.ShapeDtypeStruct((n, nh, d), F32)],
        compiler_params=_cp(("arbitrary",)),
        name="decode_attn",
    )(page_table, fq, fkn, fvn, lfn, sq, bsb, gf, gs, ck, cv, clf_t, csk, csv)


def _moe_small_kernel(h_ref, wd_ref, wg_ref, wu_ref, bg_ref, bu_ref, wdn_ref, bd_ref, x1_ref, gt_ref, g_ref,
                      y_ref, acc_scr, *, n_e, n_j):
    e = pl.program_id(0)
    j = pl.program_id(1)
    n = h_ref.shape[0]

    @pl.when(jnp.logical_and(e == 0, j == 0))
    def _():
        acc_scr[...] = jnp.zeros_like(acc_scr)

    lane = lax.broadcasted_iota(jnp.int32, (n, LANES), 1)
    we = jnp.sum(jnp.where(lane == e, wd_ref[...], 0.0), axis=1, keepdims=True)
    hb = h_ref[...].astype(BF16)
    g = jnp.minimum(_dot(hb, wg_ref[...].astype(BF16)) + bg_ref[...], SWIGLU_LIMIT)
    u = jnp.clip(_dot(hb, wu_ref[...].astype(BF16)) + bu_ref[...], -SWIGLU_LIMIT, SWIGLU_LIMIT)
    act = g * (1.0 / (1.0 + jnp.exp(-SWIGLU_ALPHA * g))) * (u + 1.0)
    y = _dot(act.astype(BF16), wdn_ref[...].astype(BF16))
    acc_scr[...] += we * y

    @pl.when(j == 0)
    def _():
        acc_scr[...] += we * bd_ref[...]

    @pl.when(jnp.logical_and(e == n_e - 1, j == n_j - 1))
    def _():
        y_ref[...] = x1_ref[...] + gt_ref[...] * _rms(acc_scr[...], g_ref[...])


def _moe_small(h2, wdense, w_gu, b_gu, w_dn, b_dn, x1, gt, g, tf=512):
    n, d = h2.shape
    n_e = w_gu.shape[0]
    f = w_gu.shape[2] // 2
    nj = f // tf
    const = lambda e, j: (0, 0)
    return pl.pallas_call(
        functools.partial(_moe_small_kernel, n_e=n_e, n_j=nj),
        grid=(n_e, nj),
        in_specs=[pl.BlockSpec((n, d), const), pl.BlockSpec((n, LANES), const),
                  pl.BlockSpec((None, d, tf), lambda e, j: (e, 0, j)),
                  pl.BlockSpec((None, d, tf), lambda e, j: (e, 0, nj + j)),
                  pl.BlockSpec((None, 1, tf), lambda e, j: (e, 0, j)),
                  pl.BlockSpec((None, 1, tf), lambda e, j: (e, 0, nj + j)),
                  pl.BlockSpec((None, tf, d), lambda e, j: (e, j, 0)),
                  pl.BlockSpec((None, 1, d), lambda e, j: (e, 0, 0)),
                  pl.BlockSpec((n, d), const), pl.BlockSpec((n, d), const), pl.BlockSpec((1, d), const)],
        out_specs=pl.BlockSpec((n, d), const),
        out_shape=jax.ShapeDtypeStruct((n, d), F32),
        scratch_shapes=[pltpu.VMEM((n, d), F32)],
        compiler_params=_cp(("arbitrary", "arbitrary")),
        name="moe_sample",
    )(h2, wdense, w_gu, w_gu, b_gu, b_gu, w_dn, b_dn, x1, gt, g)


def _pad_lanes(a, fill=0.0):
    return jnp.pad(a, ((0, 0), (0, LANES - a.shape[1])), constant_values=fill)


def kernel(x_prompt, x_sample, cache_fox_k, cache_fox_v, cache_fox_logf, cache_sb_k, cache_sb_v, page_table,
           c_prompt, c_sample, w_ada, b_ada, g_pre_mix, w_in, b_forget, b_sb, g_fox_out, g_sb_out, w_out,
           g_post_mix, g_pre_ffn, w_router, b_router, w_gate_up, b_gate_up, w_down, b_down, g_post_ffn):
    depth = w_ada.shape[0]
    assert depth == 1 and x_sample.shape[1] == 1 and x_prompt.shape[0] == 1
    bsz, seq, d = x_prompt.shape
    ns = x_sample.shape[0]
    p = bsz * seq
    dfox = N_FOX * HEAD_DIM
    dsb = N_SB * HEAD_DIM
    off_ff = 3 * dfox
    off_sq = off_ff + N_FOX
    xp = x_prompt.reshape(p, d)
    xs = x_sample.reshape(ns, d)

    n_c = bsz + ns
    c_all = jnp.concatenate([c_prompt, c_sample, jnp.zeros((-n_c % 8, d), F32)], axis=0)
    mod = _ada(c_all, w_ada[0], b_ada[0][None])
    sh_a, sc_a, gt_a, sh_m, sc_m, gt_m = [mod[0:bsz, k * d:(k + 1) * d] for k in range(6)]
    ssh_a, ssc_a, sgt_a, ssh_m, ssc_m, sgt_m = [mod[bsz:n_c, k * d:(k + 1) * d] for k in range(6)]

    w_in0 = w_in[0]
    bfor = _pad_lanes(b_forget[0][None])
    gpre = g_pre_mix[0][None]
    gf = g_fox_out[0]
    gs = g_sb_out[0]
    wr = _pad_lanes(w_router[0]).astype(BF16)
    wo = w_out[0].astype(BF16)
    br = _pad_lanes(b_router[0][None], NEG_BIG)
    gpm = g_post_mix[0][None]
    gpf = g_pre_ffn[0][None]
    gpo = g_post_ffn[0][None]

    w6 = jnp.concatenate([w_in0[:, :off_ff], w_in0[:, off_sq:]], axis=1).astype(BF16)
    wf = _pad_lanes(w_in0[:, off_ff:off_sq]).astype(BF16)
    z, zb, lf, fcum = _inproj_prompt(xp, sc_a, sh_a, gpre, w6, wf, bfor)
    tkf = 512
    nf2 = (-LOG2E * fcum[:, :N_FOX]).T.reshape(N_FOX, p // tkf, 1, tkf)
    hb = dfox // HEAD_DIM
    o_fox = _fox_attn(zb, nf2, gf[:, None, :], 0, hb, 2 * hb, tq=tkf)
    o_sb = _sb_attn(zb, b_sb[0], gs[:, None, :], 3 * hb, 4 * hb, 5 * hb)
    x1, h2, top_idx, gates, rank, _, counts = _mixout(
        o_fox, o_sb, xp, gt_a, sc_m, sh_m, gpm, gpf, wo, wr, br, tm=256)
    cnt = counts[0, :N_EXPERTS].astype(jnp.int32)
    y_p = _moe_prompt(h2, top_idx[:, :TOP_K], gates, rank[:, :TOP_K], cnt, x1, gt_m, gpo,
                      w_gate_up[0], b_gate_up[0], w_down[0], b_down[0])

    zs, lfs = _inproj_sample(xs, ssc_a, ssh_a, gpre, w_in0, bfor, off_ff)
    heads = lambda a: a.reshape(ns, -1, HEAD_DIM)
    fq_s, fk_s, fv_s = heads(zs[:, :dfox]), heads(zs[:, dfox:2 * dfox]), heads(zs[:, 2 * dfox:off_ff])
    sq_s = heads(zs[:, off_sq:off_sq + dsb])
    sk_s = heads(zs[:, off_sq + dsb:off_sq + 2 * dsb])
    sv_s = heads(zs[:, off_sq + 2 * dsb:])
    lf_s = lfs[:, :N_FOX]
    drop0 = lambda a: a.reshape(a.shape[1:])
    of_s, os_s = _decode(page_table, fq_s, fk_s, fv_s, lf_s[:, :, None], sq_s, b_sb[0][:, None], gf, gs,
                         drop0(cache_fox_k), drop0(cache_fox_v), jnp.swapaxes(drop0(cache_fox_logf), 1, 2),
                         drop0(cache_sb_k), drop0(cache_sb_v))
    x1s, h2s, _, _, _, wdense, _ = _mixout(
        of_s.reshape(ns, dfox), os_s.reshape(ns, dsb), xs, sgt_a, ssc_m, ssh_m, gpm, gpf, wo, wr, br, tm=ns)
    y_s = _moe_small(h2s, wdense, w_gate_up[0], b_gate_up[0][:, None, :], w_down[0], b_down[0][:, None, :],
                     x1s, sgt_m, gpo)

    rows5 = lambda a, nh: a.reshape(1, bsz, seq, nh, HEAD_DIM)
    srows5 = lambda a: a.reshape(1, ns, 1, -1, HEAD_DIM)
    return (y_p.reshape(bsz, seq, d), y_s.reshape(ns, 1, d),
            rows5(z[:, dfox:2 * dfox], N_FOX), rows5(z[:, 2 * dfox:3 * dfox], N_FOX),
            lf[:, :N_FOX].reshape(1, bsz, seq, N_FOX),
            rows5(z[:, 3 * dfox + dsb:3 * dfox + 2 * dsb], N_SB), rows5(z[:, 3 * dfox + 2 * dsb:], N_SB),
            srows5(fk_s), srows5(fv_s), lf_s.reshape(1, ns, 1, N_FOX), srows5(sk_s), srows5(sv_s))
```
